```python
import math, functools
import jax, jax.numpy as jnp
from jax import lax
import numpy as np

D_MODEL = 1024
BATCH = 1
SEQ = 16384
DEPTH = 4

D_FF = ((8 * D_MODEL // 3 + 127) // 128) * 128
FFN_HALF = 0.5
N_SUB = 3

ML_HEADS = 4
ML_QK = D_MODEL // 2
ML_V = D_MODEL
ML_DK = ML_QK // ML_HEADS
ML_DV = ML_V // ML_HEADS
ML_CHUNK = 64
ML_IN = 2 * ML_QK + 2 * ML_V + 2 * ML_HEADS

MB_EXPAND = 2
MB_DI = MB_EXPAND * D_MODEL
MB_HEADDIM = 64
MB_HEADS = MB_DI // MB_HEADDIM
MB_GROUPS = 4
MB_HPG = MB_HEADS // MB_GROUPS
MB_STATE = 128
MB_CONV = 4
MB_CHUNK = 128
MB_CONV_DIM = MB_DI + 2 * MB_GROUPS * MB_STATE
MB_IN = 2 * MB_DI + 2 * MB_GROUPS * MB_STATE + MB_HEADS

N_MLSTM_LAYERS = (DEPTH + 1) // 2
N_MAMBA_LAYERS = DEPTH // 2
EPS = 1e-6

kernel_name = "hybrid_mlstm_mamba2_macaron_adaln"


def rms_norm(x, g):
    xf = x.astype(jnp.float32)
    y = xf * lax.rsqrt(jnp.mean(xf * xf, axis=-1, keepdims=True) + EPS)
    return (y * g.astype(jnp.float32)).astype(x.dtype)


def _to_chunks(t, L):
    B, S = t.shape[:2]
    return jnp.moveaxis(t.reshape(B, S // L, L, *t.shape[2:]), 1, 0)


def _mlstm_chunk(carry, xs):
    C, n, m = carry
    q, k, v, ig, lf = xs
    L = q.shape[2]
    causal = jnp.tril(jnp.ones((L, L), dtype=bool))
    b = jnp.cumsum(lf, axis=-1)
    log_d = jnp.where(causal, b[..., :, None] - b[..., None, :] + ig[..., None, :], -jnp.inf)
    m_inter = b + m[..., None]
    m_t = jnp.maximum(jnp.max(log_d, axis=-1), m_inter)
    s = jnp.einsum('bhtd,bhsd->bhts', q, k) * jnp.exp(log_d - m_t[..., None])
    inter = jnp.exp(m_inter - m_t)
    num = jnp.einsum('bhts,bhsv->bhtv', s, v) + inter[..., None] * jnp.einsum('bhtd,bhdv->bhtv', q, C)
    den = jnp.sum(s, axis=-1) + inter * jnp.einsum('bhtd,bhd->bht', q, n)
    h = num / jnp.maximum(jnp.abs(den), jnp.exp(-m_t))[..., None]
    b_last = b[..., -1]
    log_w = b_last[..., None] - b + ig
    m_new = jnp.maximum(b_last + m, jnp.max(log_w, axis=-1))
    w = jnp.exp(log_w - m_new[..., None])
    decay = jnp.exp(b_last + m - m_new)
    C_new = decay[..., None, None] * C + jnp.einsum('bhs,bhsd,bhsv->bhdv', w, k, v)
    n_new = decay[..., None] * n + jnp.einsum('bhs,bhsd->bhd', w, k)
    return (C_new, n_new, m_new), h


def mlstm_mixer(h, w_in, b_gate, norm_w, w_out):
    B, S, _ = h.shape
    proj = h @ w_in
    q, k, v, o, gates = jnp.split(proj, [ML_QK, 2 * ML_QK, 2 * ML_QK + ML_V, 2 * ML_QK + 2 * ML_V], axis=-1)
    f32 = jnp.float32
    q = q.reshape(B, S, ML_HEADS, ML_DK).astype(f32)
    k = k.reshape(B, S, ML_HEADS, ML_DK).astype(f32) * (ML_DK ** -0.5)
    v = v.reshape(B, S, ML_HEADS, ML_DV).astype(f32)
    gates = gates.astype(f32) + b_gate.astype(f32)
    ig, fg = jnp.split(gates, 2, axis=-1)
    lf = jax.nn.log_sigmoid(fg)
    xs = tuple(jnp.swapaxes(_to_chunks(t, ML_CHUNK), 2, 3) for t in (q, k, v, ig, lf))
    carry0 = (jnp.zeros((B, ML_HEADS, ML_DK, ML_DV), f32),
              jnp.zeros((B, ML_HEADS, ML_DK), f32),
              jnp.zeros((B, ML_HEADS), f32))
    _, hs = lax.scan(_mlstm_chunk, carry0, xs)
    hs = jnp.moveaxis(jnp.swapaxes(hs, 2, 3), 0, 1).reshape(B, S, ML_HEADS, ML_DV)
    hs = hs * lax.rsqrt(jnp.mean(hs * hs, axis=-1, keepdims=True) + EPS)
    hs = hs.reshape(B, S, ML_V) * norm_w.astype(f32) * jax.nn.sigmoid(o.astype(f32))
    return hs.astype(h.dtype) @ w_out


def _ssd_chunk(state, xs, A):
    x, dt, Bc, Cc = xs
    Q = x.shape[1]
    causal = jnp.tril(jnp.ones((Q, Q), dtype=bool))
    a = jnp.cumsum(dt * A, axis=1)
    seg = jnp.where(causal[None, :, :, None, None], a[:, :, None] - a[:, None, :], -jnp.inf)
    w = jnp.einsum('btgn,bsgn->btsg', Cc, Bc)[..., None] * jnp.exp(seg) * dt[:, None]
    y = (jnp.einsum('btsgh,bsghp->btghp', w, x)
         + jnp.exp(a)[..., None] * jnp.einsum('btgn,bghpn->btghp', Cc, state))
    a_last = a[:, -1]
    ws = jnp.exp(a_last[:, None] - a) * dt
    state_new = (jnp.exp(a_last)[..., None, None] * state
                 + jnp.einsum('bsgn,bsgh,bsghp->bghpn', Bc, ws, x))
    return state_new, y


def mamba2_mixer(h, w_in, conv_w, conv_b, dt_bias, A_log, D_skip, norm_w, w_out):
    B, S, _ = h.shape
    f32 = jnp.float32
    proj = h @ w_in
    z, xbc, dt = jnp.split(proj, [MB_DI, MB_DI + MB_CONV_DIM], axis=-1)
    xbc = lax.conv_general_dilated(xbc, conv_w[:, None, :].astype(xbc.dtype), window_strides=(1,),
                                   padding=[(MB_CONV - 1, 0)], dimension_numbers=('NWC', 'WIO', 'NWC'),
                                   feature_group_count=MB_CONV_DIM)
    xbc = jax.nn.silu((xbc + conv_b).astype(f32))
    xs, Bm, Cm = jnp.split(xbc, [MB_DI, MB_DI + MB_GROUPS * MB_STATE], axis=-1)
    xs = xs.reshape(B, S, MB_GROUPS, MB_HPG, MB_HEADDIM)
    Bm = Bm.reshape(B, S, MB_GROUPS, MB_STATE)
    Cm = Cm.reshape(B, S, MB_GROUPS, MB_STATE)
    dt = jax.nn.softplus(dt.astype(f32) + dt_bias.astype(f32)).reshape(B, S, MB_GROUPS, MB_HPG)
    A = -jnp.exp(A_log.astype(f32)).reshape(MB_GROUPS, MB_HPG)
    chunks = tuple(_to_chunks(t, MB_CHUNK) for t in (xs, dt, Bm, Cm))
    state0 = jnp.zeros((B, MB_GROUPS, MB_HPG, MB_HEADDIM, MB_STATE), f32)
    _, ys = lax.scan(functools.partial(_ssd_chunk, A=A), state0, chunks)
    y = jnp.moveaxis(ys, 0, 1).reshape(B, S, MB_GROUPS, MB_HPG, MB_HEADDIM)
    y = y + D_skip.astype(f32).reshape(MB_GROUPS, MB_HPG)[..., None] * xs
    y = y.reshape(B, S, MB_DI) * jax.nn.silu(z.astype(f32))
    y = y.reshape(B, S, MB_GROUPS, MB_DI // MB_GROUPS)
    y = y * lax.rsqrt(jnp.mean(y * y, axis=-1, keepdims=True) + EPS)
    y = y.reshape(B, S, MB_DI) * norm_w.astype(f32)
    return y.astype(h.dtype) @ w_out


def swiglu(h, w1, w3, w2):
    return (jax.nn.silu(h @ w1) * (h @ w3)) @ w2


def sublayer(x, fn, g_pre, g_post, shift, scale, gate, weight):
    hmod = rms_norm(x, g_pre) * (1 + scale[:, None, :]) + shift[:, None, :]
    y = rms_norm(fn(hmod), g_post)
    return x + weight * gate[:, None, :] * y


def setup_inputs(seed: int = 0) -> dict:
    key = jax.random.key(seed)
    ks = jax.random.split(key, 24)
    D = D_MODEL
    nrm = lambda k, shape, fan_in: jax.random.normal(k, shape, jnp.float32) * (fan_in ** -0.5)
    u_dt = jax.random.uniform(ks[17], (N_MAMBA_LAYERS, MB_HEADS), jnp.float32)
    dt0 = jnp.exp(u_dt * (math.log(0.1) - math.log(1e-3)) + math.log(1e-3))
    ig_b = 0.5 * jax.random.normal(ks[12], (N_MLSTM_LAYERS, ML_HEADS), jnp.float32)
    fg_b = 3.0 + 3.0 * jax.random.uniform(ks[13], (N_MLSTM_LAYERS, ML_HEADS), jnp.float32)
    return {
        "x": jax.random.normal(ks[0], (BATCH, SEQ, D), jnp.float32),
        "c": jax.random.normal(ks[1], (BATCH, D), jnp.float32),
        "ada_w": 0.5 * nrm(ks[2], (DEPTH, D, 3 * N_SUB * D), D),
        "ada_b": 0.02 * jax.random.normal(ks[3], (DEPTH, 3 * N_SUB * D), jnp.float32),
        "norm_pre": 1.0 + 0.1 * jax.random.normal(ks[4], (DEPTH, N_SUB, D), jnp.float32),
        "norm_post": 1.0 + 0.1 * jax.random.normal(ks[5], (DEPTH, N_SUB, D), jnp.float32),
        "ffn_w1": nrm(ks[6], (DEPTH, 2, D, D_FF), D),
        "ffn_w3": nrm(ks[7], (DEPTH, 2, D, D_FF), D),
        "ffn_w2": nrm(ks[8], (DEPTH, 2, D_FF, D), D_FF),
        "ml_w_in": nrm(ks[9], (N_MLSTM_LAYERS, D, ML_IN), D),
        "ml_b_gate": jnp.concatenate([ig_b, fg_b], axis=-1),
        "ml_norm_w": 1.0 + 0.1 * jax.random.normal(ks[10], (N_MLSTM_LAYERS, ML_V), jnp.float32),
        "ml_w_out": nrm(ks[11], (N_MLSTM_LAYERS, ML_V, D), ML_V),
        "mb_w_in": nrm(ks[14], (N_MAMBA_LAYERS, D, MB_IN), D),
        "mb_conv_w": nrm(ks[15], (N_MAMBA_LAYERS, MB_CONV, MB_CONV_DIM), MB_CONV),
        "mb_conv_b": 0.02 * jax.random.normal(ks[16], (N_MAMBA_LAYERS, MB_CONV_DIM), jnp.float32),
        "mb_dt_bias": dt0 + jnp.log(-jnp.expm1(-dt0)),
        "mb_A_log": jnp.log(jax.random.uniform(ks[18], (N_MAMBA_LAYERS, MB_HEADS), jnp.float32, 1.0, 16.0)),
        "mb_D": 1.0 + 0.1 * jax.random.normal(ks[19], (N_MAMBA_LAYERS, MB_HEADS), jnp.float32),
        "mb_norm_w": 1.0 + 0.1 * jax.random.normal(ks[20], (N_MAMBA_LAYERS, MB_DI), jnp.float32),
        "mb_w_out": nrm(ks[21], (N_MAMBA_LAYERS, MB_DI, D), MB_DI),
    }


def reference(x, c, ada_w, ada_b, norm_pre, norm_post, ffn_w1, ffn_w3, ffn_w2,
              ml_w_in, ml_b_gate, ml_norm_w, ml_w_out,
              mb_w_in, mb_conv_w, mb_conv_b, mb_dt_bias, mb_A_log, mb_D, mb_norm_w, mb_w_out):
    B = x.shape[0]
    c_act = jax.nn.silu(c)
    for i in range(DEPTH):
        mod = (c_act @ ada_w[i] + ada_b[i]).reshape(B, N_SUB, 3, D_MODEL)
        j = i // 2
        if i % 2 == 0:
            mixer = lambda h, j=j: mlstm_mixer(h, ml_w_in[j], ml_b_gate[j], ml_norm_w[j], ml_w_out[j])
        else:
            mixer = lambda h, j=j: mamba2_mixer(h, mb_w_in[j], mb_conv_w[j], mb_conv_b[j], mb_dt_bias[j],
                                                mb_A_log[j], mb_D[j], mb_norm_w[j], mb_w_out[j])
        fns = (lambda h, i=i: swiglu(h, ffn_w1[i, 0], ffn_w3[i, 0], ffn_w2[i, 0]),
               mixer,
               lambda h, i=i: swiglu(h, ffn_w1[i, 1], ffn_w3[i, 1], ffn_w2[i, 1]))
        weights = (FFN_HALF, 1.0, FFN_HALF)
        for s in range(N_SUB):
            x = sublayer(x, fns[s], norm_pre[i, s], norm_post[i, s],
                         mod[:, s, 0], mod[:, s, 1], mod[:, s, 2], weights[s])
    return x
```

```python
import functools

import jax
import jax.numpy as jnp
from jax import lax
from jax.experimental import pallas as pl
from jax.experimental.pallas import tpu as pltpu

F32 = jnp.float32
BF16 = jnp.bfloat16
HIGHEST = lax.Precision.HIGHEST
EPS = 1e-6

N_SUB = 3
FFN_HALF = 0.5
ML_HEADS = 4
MB_HEADDIM = 64
MB_GROUPS = 4
MB_STATE = 128
MB_CONV = 4

LANES = 128
MXU_N = 256
ROW_TILE = 512
CHUNK = 128
CONV_HALO = 8
VMEM_LIMIT = 56 * 1024 * 1024

_NT = (((1,), (1,)), ((), ()))


def _dot(a, b, precision=None):
    return jnp.dot(a, b, preferred_element_type=F32, precision=precision)


def _dot_nt(a, b):
    return lax.dot_general(a, b, _NT, preferred_element_type=F32)


def _sigmoid(v):
    return 1.0 / (1.0 + jnp.exp(-v))


def _silu(v):
    return v * _sigmoid(v)


def _softplus(v):
    return jnp.maximum(v, 0.0) + jnp.log1p(jnp.exp(-jnp.abs(v)))


def _log_sigmoid(v):
    return -_softplus(-v)


def _prenorm_mod(x, vec):
    xn = x * lax.rsqrt(jnp.mean(x * x, axis=-1, keepdims=True) + EPS)
    return (xn * vec[0:1]) * (1.0 + vec[2:3]) + vec[1:2]


def _postnorm_residual(x, y, vec, weight):
    yn = y * lax.rsqrt(jnp.mean(y * y, axis=-1, keepdims=True) + EPS) * vec[3:4]
    return x + (weight * vec[4:5]) * yn


def _const_spec(shape):
    zeros = (0,) * len(shape)
    return pl.BlockSpec(shape, lambda i: zeros, pipeline_mode=pl.Buffered(1))


def _row_spec(rows, cols):
    return pl.BlockSpec((rows, cols), lambda i: (i, 0))


def _col_spec(rows, cols):
    return pl.BlockSpec((rows, cols), lambda i: (0, i))


def _params():
    return pltpu.CompilerParams(dimension_semantics=("arbitrary",), vmem_limit_bytes=VMEM_LIMIT)


def _ada_kernel(c_ref, w_ref, b_ref, o_ref):
    c = c_ref[...]
    o_ref[0] = _dot(_silu(c), w_ref[0], HIGHEST) + b_ref[0]


def _ada_mod(c, ada_w, ada_b):
    depth, d, n = ada_w.shape
    tn = n // 6
    c8 = jnp.broadcast_to(c, (8, d))
    out = pl.pallas_call(
        _ada_kernel,
        grid=(depth, n // tn),
        in_specs=[pl.BlockSpec((8, d), lambda i, j: (0, 0)),
                  pl.BlockSpec((1, d, tn), lambda i, j: (i, 0, j)),
                  pl.BlockSpec((1, 1, tn), lambda i, j: (i, 0, j))],
        out_specs=pl.BlockSpec((1, 8, tn), lambda i, j: (i, 0, j)),
        out_shape=jax.ShapeDtypeStruct((depth, 8, n), F32),
        compiler_params=pltpu.CompilerParams(dimension_semantics=("arbitrary", "arbitrary"),
                                             vmem_limit_bytes=VMEM_LIMIT),
        name="ada_mod",
    )(c8, ada_w, ada_b.reshape(depth, 1, n))
    return out[:, 0, :]


def _ffn_kernel(x_ref, vec_ref, w1_ref, w3_ref, w2_ref, o_ref, h_ref, act_ref, *, weight, d_ff):
    x = x_ref[...]
    vec = vec_ref[...]
    h_ref[...] = _prenorm_mod(x, vec).astype(BF16)
    for j in range(d_ff // MXU_N):
        cols = slice(j * MXU_N, (j + 1) * MXU_N)
        h = h_ref[...]
        a = _dot(h, w1_ref[:, cols])
        b = _dot(h, w3_ref[:, cols])
        act_ref[:, cols] = (_silu(a) * b).astype(BF16)
    y = _dot(act_ref[...], w2_ref[...])
    o_ref[...] = _postnorm_residual(x, y, vec, weight)


def _ffn_sublayer(x, vec, w1, w3, w2, weight):
    s, d = x.shape
    d_ff = w1.shape[1]
    tm = min(ROW_TILE, s)
    return pl.pallas_call(
        functools.partial(_ffn_kernel, weight=weight, d_ff=d_ff),
        grid=(s // tm,),
        in_specs=[_row_spec(tm, d), _const_spec((8, d)), _const_spec((d, d_ff)),
                  _const_spec((d, d_ff)), _const_spec((d_ff, d))],
        out_specs=_row_spec(tm, d),
        out_shape=jax.ShapeDtypeStruct((s, d), F32),
        scratch_shapes=[pltpu.VMEM((tm, d), BF16), pltpu.VMEM((tm, d_ff), BF16)],
        compiler_params=_params(),
        name="ffn",
    )(x, vec, w1, w3, w2)


def _out_kernel(y_ref, x_ref, vec_ref, w_ref, o_ref):
    y = _dot(y_ref[...], w_ref[...])
    o_ref[...] = _postnorm_residual(x_ref[...], y, vec_ref[...], 1.0)


def _out_proj(y, x, vec, w):
    s, d = x.shape
    k = y.shape[1]
    tm = min(ROW_TILE, s)
    return pl.pallas_call(
        _out_kernel,
        grid=(s // tm,),
        in_specs=[_row_spec(tm, k), _row_spec(tm, d), _const_spec((8, d)), _const_spec((k, d))],
        out_specs=_row_spec(tm, d),
        out_shape=jax.ShapeDtypeStruct((s, d), F32),
        compiler_params=_params(),
        name="out_proj",
    )(y, x, vec, w)


def _ml_in_kernel(x_ref, vec_ref, wm_ref, wkt_ref, wg_ref, wgt_ref, bc_ref, br_ref,
                  qvo_ref, kt_ref, gcol_ref, grow_ref, *, k_scale):
    h = _prenorm_mod(x_ref[...], vec_ref[...]).astype(BF16)
    n_main = wm_ref.shape[1]
    for j in range(n_main // 512):
        cols = slice(j * 512, (j + 1) * 512)
        qvo_ref[:, cols] = _dot(h, wm_ref[:, cols]).astype(BF16)
    kt_ref[...] = (_dot_nt(wkt_ref[...], h) * k_scale).astype(BF16)
    gcol_ref[...] = _dot(h, wg_ref[...]) + bc_ref[...]
    grow_ref[...] = _dot_nt(wgt_ref[...], h) + br_ref[...]


def _ml_core_kernel(q_ref, kt_ref, v_ref, o_ref, gcol_ref, grow_ref, nw_ref, out_ref,
                    c_ref, m_ref, *, heads, dk, dv):
    L = CHUNK

    @pl.when(pl.program_id(0) == 0)
    def _():
        c_ref[...] = jnp.zeros_like(c_ref)
        m_ref[...] = jnp.zeros_like(m_ref)

    row = lax.broadcasted_iota(jnp.int32, (L, L), 0)
    col = lax.broadcasted_iota(jnp.int32, (L, L), 1)
    causal = col <= row
    tril = jnp.where(causal, 1.0, 0.0).astype(F32)
    triu = jnp.where(row <= col, 1.0, 0.0).astype(F32)

    gc = gcol_ref[...]
    gr = grow_ref[...]
    lane = lax.broadcasted_iota(jnp.int32, gc.shape, 1)
    sub = lax.broadcasted_iota(jnp.int32, gr.shape, 0)
    lf_c = jnp.where((lane >= heads) & (lane < 2 * heads), _log_sigmoid(gc), 0.0)
    lf_r = jnp.where((sub >= heads) & (sub < 2 * heads), _log_sigmoid(gr), 0.0)
    b_col = _dot(tril, lf_c, HIGHEST)
    b_row = _dot(lf_r, triu, HIGHEST)

    ones_col = jnp.where(lax.broadcasted_iota(jnp.int32, (L, LANES), 1) == 0, 1.0, 0.0).astype(BF16)

    for h in range(heads):
        q = q_ref[:, h * dk:(h + 1) * dk]
        kt = kt_ref[h * dk:(h + 1) * dk, :]
        v_aug = jnp.concatenate([v_ref[:, h * dv:(h + 1) * dv], ones_col], axis=1)
        b_c = b_col[:, heads + h:heads + h + 1]
        ig_c = gc[:, h:h + 1]
        b_r = b_row[heads + h:heads + h + 1, :]
        ig_r = gr[h:h + 1, :]
        m_prev = m_ref[h:h + 1, 0:1]
        b_last = b_r[:, L - 1:L]

        log_d = jnp.where(causal, b_c - b_r + ig_r, -jnp.inf)
        m_inter = b_c + m_prev
        m_t = jnp.maximum(jnp.max(log_d, axis=1, keepdims=True), m_inter)
        s = _dot(q, kt) * jnp.exp(log_d - m_t)
        inter = jnp.exp(m_inter - m_t)
        c_old = c_ref[h]
        numden = _dot(s.astype(BF16), v_aug) + inter * _dot(q, c_old.astype(BF16))
        num = numden[:, :dv]
        den = numden[:, dv:dv + 1]
        hh = num / jnp.maximum(jnp.abs(den), jnp.exp(-m_t))

        log_w = b_last - b_c + ig_c
        m_new = jnp.maximum(b_last + m_prev, jnp.max(log_w, axis=0, keepdims=True))
        w = jnp.exp(log_w - m_new)
        decay = jnp.exp(b_last + m_prev - m_new)
        wv = (w * v_aug.astype(F32)).astype(BF16)
        c_ref[h] = decay * c_old + _dot(kt, wv)
        m_ref[h:h + 1, :] = jnp.broadcast_to(m_new, (1, LANES))

        hn = hh * lax.rsqrt(jnp.mean(hh * hh, axis=-1, keepdims=True) + EPS)
        gate = _sigmoid(o_ref[:, h * dv:(h + 1) * dv].astype(F32))
        out_ref[:, h * dv:(h + 1) * dv] = (hn * nw_ref[:, h * dv:(h + 1) * dv] * gate).astype(BF16)


def _mlstm_sublayer(x, vec, w_in, b_gate, norm_w, w_out):
    s, d = x.shape
    heads = ML_HEADS
    qk = (w_in.shape[1] - 2 * heads) // 6
    dvt = 2 * qk
    dk, dv = qk // heads, dvt // heads
    tm = min(ROW_TILE, s)

    wq, wk, wv, wo, wgate = jnp.split(w_in, [qk, 2 * qk, 2 * qk + dvt, 2 * qk + 2 * dvt], axis=1)
    wm = jnp.concatenate([wv, wo, wq], axis=1).astype(BF16)
    wkt = wk.T.astype(BF16)
    wg = jnp.pad(wgate, ((0, 0), (0, LANES - 2 * heads))).astype(BF16)
    wgt = jnp.pad(wgate.T, ((0, 16 - 2 * heads), (0, 0))).astype(BF16)
    bc = jnp.pad(b_gate, (0, LANES - 2 * heads)).reshape(1, LANES)
    br = jnp.pad(b_gate, (0, 16 - 2 * heads)).reshape(16, 1)
    n_main = wm.shape[1]

    qvo, kt, gcol, grow = pl.pallas_call(
        functools.partial(_ml_in_kernel, k_scale=dk ** -0.5),
        grid=(s // tm,),
        in_specs=[_row_spec(tm, d), _const_spec((8, d)), _const_spec((d, n_main)),
                  _const_spec((qk, d)), _const_spec((d, LANES)), _const_spec((16, d)),
                  _const_spec((1, LANES)), _const_spec((16, 1))],
        out_specs=[_row_spec(tm, n_main), _col_spec(qk, tm), _row_spec(tm, LANES), _col_spec(16, tm)],
        out_shape=[jax.ShapeDtypeStruct((s, n_main), BF16), jax.ShapeDtypeStruct((qk, s), BF16),
                   jax.ShapeDtypeStruct((s, LANES), F32), jax.ShapeDtypeStruct((16, s), F32)],
        compiler_params=_params(),
        name="mlstm_in",
    )(x, vec, wm, wkt, wg, wgt, bc, br)

    L = CHUNK
    hs = pl.pallas_call(
        functools.partial(_ml_core_kernel, heads=heads, dk=dk, dv=dv),
        grid=(s // L,),
        in_specs=[pl.BlockSpec((L, qk), lambda i: (i, 2 * dvt // qk)),
                  _col_spec(qk, L),
                  pl.BlockSpec((L, dvt), lambda i: (i, 0)),
                  pl.BlockSpec((L, dvt), lambda i: (i, 1)),
                  _row_spec(L, LANES), _col_spec(16, L), _const_spec((1, dvt))],
        out_specs=_row_spec(L, dvt),
        out_shape=jax.ShapeDtypeStruct((s, dvt), BF16),
        scratch_shapes=[pltpu.VMEM((heads, dk, dv + LANES), F32), pltpu.VMEM((8, LANES), F32)],
        compiler_params=_params(),
        name="mlstm_core",
    )(qvo, kt, qvo, qvo, gcol, grow, norm_w.reshape(1, dvt))
    return _out_proj(hs, x, vec, w_out.astype(BF16))


def _mb_in_kernel(x_ref, vec_ref, wz_ref, wx_ref, wd_ref, wdt_ref, z_ref, xbc_ref, dtc_ref, dtr_ref):
    h = _prenorm_mod(x_ref[...], vec_ref[...]).astype(BF16)
    for j in range(wz_ref.shape[1] // 512):
        cols = slice(j * 512, (j + 1) * 512)
        z_ref[:, cols] = _dot(h, wz_ref[:, cols]).astype(BF16)
    for j in range(wx_ref.shape[1] // 512):
        cols = slice(j * 512, (j + 1) * 512)
        xbc_ref[:, cols] = _dot(h, wx_ref[:, cols]).astype(BF16)
    dtc_ref[...] = _dot(h, wd_ref[...])
    dtr_ref[...] = _dot_nt(wdt_ref[...], h)


def _mb_core_kernel(z_ref, xbc_ref, dtc_ref, dtr_ref, cw_ref, cb_ref, dbc_ref, dbr_ref, ac_ref, ar_ref,
                    e_ref, dsk_ref, nw_ref, y_ref, xbuf_ref, xa_ref, st_ref, *, di, groups, nstate, hpg, p):
    Q = CHUNK
    gw = hpg * p
    conv_dim = di + 2 * groups * nstate

    @pl.when(pl.program_id(0) == 0)
    def _():
        xbuf_ref[0:CONV_HALO, :] = jnp.zeros((CONV_HALO, conv_dim), F32)
        st_ref[...] = jnp.zeros_like(st_ref)

    for j in range(conv_dim // 512):
        cols = slice(j * 512, (j + 1) * 512)
        xbuf_ref[CONV_HALO:CONV_HALO + Q, cols] = xbc_ref[:, cols].astype(F32)
        acc = cb_ref[:, cols] + cw_ref[0:1, cols] * xbuf_ref[CONV_HALO - 3:CONV_HALO - 3 + Q, cols]
        for k in range(1, MB_CONV):
            off = CONV_HALO - (MB_CONV - 1) + k
            acc = acc + cw_ref[k:k + 1, cols] * xbuf_ref[off:off + Q, cols]
        xa_ref[:, cols] = _silu(acc)
        xbuf_ref[0:CONV_HALO, cols] = xbuf_ref[Q:Q + CONV_HALO, cols]

    row = lax.broadcasted_iota(jnp.int32, (Q, Q), 0)
    col = lax.broadcasted_iota(jnp.int32, (Q, Q), 1)
    causal = col <= row
    tril = jnp.where(causal, 1.0, 0.0).astype(F32)
    triu = jnp.where(row <= col, 1.0, 0.0).astype(F32)

    dt_c = _softplus(dtc_ref[...] + dbc_ref[...])
    dt_r = _softplus(dtr_ref[...] + dbr_ref[...])
    a_c = _dot(tril, dt_c * -jnp.exp(ac_ref[...]), HIGHEST)
    a_r = _dot(dt_r * -jnp.exp(ar_ref[...]), triu, HIGHEST)
    a_last = a_c[Q - 1:Q, :]
    ea_c = jnp.exp(a_c)
    dec_c = jnp.exp(a_last - a_c)
    eal_c = jnp.broadcast_to(jnp.exp(a_last), (8, LANES))
    lane = lax.broadcasted_iota(jnp.int32, (Q, LANES), 1)

    for g in range(groups):
        gcols = slice(g * gw, (g + 1) * gw)
        e_g = e_ref[:, gcols]
        xs = xa_ref[:, gcols]
        bm = xa_ref[:, di + g * nstate:di + (g + 1) * nstate]
        cm = xa_ref[:, di + groups * nstate + g * nstate:di + groups * nstate + (g + 1) * nstate]
        cm16 = cm.astype(BF16)
        xdt = xs * _dot(dt_c, e_g, HIGHEST)
        xdt16 = xdt.astype(BF16)
        cb = jnp.where(causal, _dot_nt(cm16, bm.astype(BF16)), 0.0)

        pairs = []
        for hp in range(hpg // 2):
            x_pair = xdt16[:, hp * LANES:(hp + 1) * LANES]
            ys = []
            for e in range(2):
                hd = g * hpg + hp * 2 + e
                seg = a_c[:, hd:hd + 1] - a_r[hd:hd + 1, :]
                lmat = (cb * jnp.exp(jnp.minimum(seg, 0.0))).astype(BF16)
                ys.append(_dot(lmat, x_pair))
            pairs.append(jnp.where(lane < p, ys[0], ys[1]))
        y = jnp.concatenate(pairs, axis=1)

        st = st_ref[g]
        y = y + _dot(cm16, st.astype(BF16)) * _dot(ea_c, e_g, HIGHEST)
        y = y + dsk_ref[:, gcols] * xs

        xdec = (xdt * _dot(dec_c, e_g, HIGHEST)).astype(BF16)
        eal = _dot(eal_c, e_g, HIGHEST)[0:1, :]
        st_ref[g] = eal * st + _dot(bm.T.astype(BF16), xdec)

        y = y * _silu(z_ref[:, gcols].astype(F32))
        y = y * lax.rsqrt(jnp.mean(y * y, axis=-1, keepdims=True) + EPS)
        y_ref[:, gcols] = (y * nw_ref[:, gcols]).astype(BF16)


def _mamba_sublayer(x, vec, w_in, conv_w, conv_b, dt_bias, a_log, d_skip, norm_w, w_out):
    s, d = x.shape
    heads = dt_bias.shape[0]
    p, groups, nstate = MB_HEADDIM, MB_GROUPS, MB_STATE
    hpg = heads // groups
    di = heads * p
    conv_dim = di + 2 * groups * nstate
    tm = min(ROW_TILE, s)

    wz = w_in[:, :di].astype(BF16)
    wx = w_in[:, di:di + conv_dim].astype(BF16)
    wdt = w_in[:, di + conv_dim:]
    wd = jnp.pad(wdt, ((0, 0), (0, LANES - heads))).astype(BF16)
    wdtt = wdt.T.astype(BF16)

    z, xbc, dtc, dtr = pl.pallas_call(
        _mb_in_kernel,
        grid=(s // tm,),
        in_specs=[_row_spec(tm, d), _const_spec((8, d)), _const_spec((d, di)), _const_spec((d, conv_dim)),
                  _const_spec((d, LANES)), _const_spec((heads, d))],
        out_specs=[_row_spec(tm, di), _row_spec(tm, conv_dim), _row_spec(tm, LANES), _col_spec(heads, tm)],
        out_shape=[jax.ShapeDtypeStruct((s, di), BF16), jax.ShapeDtypeStruct((s, conv_dim), BF16),
                   jax.ShapeDtypeStruct((s, LANES), F32), jax.ShapeDtypeStruct((heads, s), F32)],
        compiler_params=_params(),
        name="mamba_in",
    )(x, vec, wz, wx, wd, wdtt)

    pad = (0, LANES - heads)
    expand = (jnp.arange(LANES)[:, None] == (jnp.arange(di)[None, :] // p)).astype(F32)
    Q = CHUNK
    y = pl.pallas_call(
        functools.partial(_mb_core_kernel, di=di, groups=groups, nstate=nstate, hpg=hpg, p=p),
        grid=(s // Q,),
        in_specs=[_row_spec(Q, di), _row_spec(Q, conv_dim), _row_spec(Q, LANES), _col_spec(heads, Q),
                  _const_spec((MB_CONV, conv_dim)), _const_spec((1, conv_dim)),
                  _const_spec((1, LANES)), _const_spec((heads, 1)),
                  _const_spec((1, LANES)), _const_spec((heads, 1)),
                  _const_spec((LANES, di)), _const_spec((1, di)), _const_spec((1, di))],
        out_specs=_row_spec(Q, di),
        out_shape=jax.ShapeDtypeStruct((s, di), BF16),
        scratch_shapes=[pltpu.VMEM((Q + CONV_HALO, conv_dim), F32), pltpu.VMEM((Q, conv_dim), F32),
                        pltpu.VMEM((groups, nstate, hpg * p), F32)],
        compiler_params=_params(),
        name="mamba_core",
    )(z, xbc, dtc, dtr, conv_w, conv_b.reshape(1, conv_dim),
      jnp.pad(dt_bias, pad).reshape(1, LANES), dt_bias.reshape(heads, 1),
      jnp.pad(a_log, pad).reshape(1, LANES), a_log.reshape(heads, 1),
      expand, jnp.repeat(d_skip, p).reshape(1, di), norm_w.reshape(1, di))
    return _out_proj(y, x, vec, w_out.astype(BF16))


def kernel(x, c, ada_w, ada_b, norm_pre, norm_post, ffn_w1, ffn_w3, ffn_w2, ml_w_in, ml_b_gate, ml_norm_w,
           ml_w_out, mb_w_in, mb_conv_w, mb_conv_b, mb_dt_bias, mb_A_log, mb_D, mb_norm_w, mb_w_out):
    batch, seq, d = x.shape
    assert batch == 1, "the kernels assume one sequence"
    depth = ada_w.shape[0]
    mod = _ada_mod(c, ada_w, ada_b).reshape(depth, N_SUB, 3, d)
    pad_rows = jnp.zeros((3, d), F32)
    h = x.reshape(seq, d)
    for i in range(depth):
        j = i // 2
        for sub in range(N_SUB):
            vec = jnp.concatenate([norm_pre[i, sub][None], mod[i, sub, 0][None], mod[i, sub, 1][None],
                                   norm_post[i, sub][None], mod[i, sub, 2][None], pad_rows], axis=0)
            if sub != 1:
                f = sub // 2
                h = _ffn_sublayer(h, vec, ffn_w1[i, f].astype(BF16), ffn_w3[i, f].astype(BF16),
                                  ffn_w2[i, f].astype(BF16), FFN_HALF)
            elif i % 2 == 0:
                h = _mlstm_sublayer(h, vec, ml_w_in[j], ml_b_gate[j], ml_norm_w[j], ml_w_out[j])
            else:
                h = _mamba_sublayer(h, vec, mb_w_in[j], mb_conv_w[j], mb_conv_b[j], mb_dt_bias[j],
                                    mb_A_log[j], mb_D[j], mb_norm_w[j], mb_w_out[j])
    return h.reshape(batch, seq, d)
```

```python
import functools

import jax
import jax.numpy as jnp
from jax import lax
from jax.experimental import pallas as pl
from jax.experimental.pallas import tpu as pltpu

F32 = jnp.float32
BF16 = jnp.bfloat16
HIGHEST = lax.Precision.HIGHEST
EPS = 1e-6

N_SUB = 3
FFN_HALF = 0.5
ML_HEADS = 4
MB_HEADDIM = 64
MB_GROUPS = 4
MB_STATE = 128
MB_CONV = 4

LANES = 128
MXU_N = 256
ROW_TILE = 512
CHUNK = 128
CONV_HALO = 8
VMEM_LIMIT = 56 * 1024 * 1024

_NT = (((1,), (1,)), ((), ()))


def _dot(a, b, precision=None):
    return jnp.dot(a, b, preferred_element_type=F32, precision=precision)


def _dot_nt(a, b):
    return lax.dot_general(a, b, _NT, preferred_element_type=F32)


def _sigmoid(v):
    return 1.0 / (1.0 + jnp.exp(-v))


def _silu(v):
    return v * _sigmoid(v)


def _softplus(v):
    return jnp.maximum(v, 0.0) + jnp.log1p(jnp.exp(-jnp.abs(v)))


def _log_sigmoid(v):
    return -_softplus(-v)


def _prenorm_mod(x, vec):
    xn = x * lax.rsqrt(jnp.mean(x * x, axis=-1, keepdims=True) + EPS)
    return (xn * vec[0:1]) * (1.0 + vec[2:3]) + vec[1:2]


def _postnorm_residual(x, y, vec, weight):
    yn = y * lax.rsqrt(jnp.mean(y * y, axis=-1, keepdims=True) + EPS) * vec[3:4]
    return x + (weight * vec[4:5]) * yn


def _const_spec(shape):
    zeros = (0,) * len(shape)
    return pl.BlockSpec(shape, lambda i: zeros, pipeline_mode=pl.Buffered(1))


def _layer_spec(layer, rows, cols, col_block):
    return pl.BlockSpec((None, rows, cols), lambda i: (layer, 0, col_block), pipeline_mode=pl.Buffered(1))


def _row_spec(rows, cols):
    return pl.BlockSpec((rows, cols), lambda i: (i, 0))


def _col_spec(rows, cols):
    return pl.BlockSpec((rows, cols), lambda i: (0, i))


def _params():
    return pltpu.CompilerParams(dimension_semantics=("arbitrary",), vmem_limit_bytes=VMEM_LIMIT)


def _ada_kernel(c_ref, w_ref, b_ref, o_ref):
    c = c_ref[...]
    o_ref[0] = _dot(_silu(c), w_ref[0], HIGHEST) + b_ref[0]


def _ada_mod(c, ada_w, ada_b):
    depth, d, n = ada_w.shape
    tn = n // 6
    c8 = jnp.broadcast_to(c, (8, d))
    out = pl.pallas_call(
        _ada_kernel,
        grid=(depth, n // tn),
        in_specs=[pl.BlockSpec((8, d), lambda i, j: (0, 0)),
                  pl.BlockSpec((1, d, tn), lambda i, j: (i, 0, j)),
                  pl.BlockSpec((1, 1, tn), lambda i, j: (i, 0, j))],
        out_specs=pl.BlockSpec((1, 8, tn), lambda i, j: (i, 0, j)),
        out_shape=jax.ShapeDtypeStruct((depth, 8, n), F32),
        compiler_params=pltpu.CompilerParams(dimension_semantics=("arbitrary", "arbitrary"),
                                             vmem_limit_bytes=VMEM_LIMIT),
        name="ada_mod",
    )(c8, ada_w, ada_b.reshape(depth, 1, n))
    return out[:, 0, :]


def _ffn_kernel(x_ref, vec_ref, w1_ref, w3_ref, w2_ref, o_ref, h_ref, act_ref, *, weight, d_ff):
    x = x_ref[...]
    vec = vec_ref[...]
    h_ref[...] = _prenorm_mod(x, vec).astype(BF16)
    for j in range(d_ff // MXU_N):
        cols = slice(j * MXU_N, (j + 1) * MXU_N)
        h = h_ref[...]
        a = _dot(h, w1_ref[:, cols])
        b = _dot(h, w3_ref[:, cols])
        act_ref[:, cols] = (_silu(a) * b).astype(BF16)
    y = _dot(act_ref[...], w2_ref[...])
    o_ref[...] = _postnorm_residual(x, y, vec, weight)


def _ffn_sublayer(x, vec, w1, w3, w2, layer, weight):
    s, d = x.shape
    d_ff = w1.shape[2]
    tm = min(ROW_TILE, s)
    return pl.pallas_call(
        functools.partial(_ffn_kernel, weight=weight, d_ff=d_ff),
        grid=(s // tm,),
        in_specs=[_row_spec(tm, d), _const_spec((8, d)), _layer_spec(layer, d, d_ff, 0),
                  _layer_spec(layer, d, d_ff, 0), _layer_spec(layer, d_ff, d, 0)],
        out_specs=_row_spec(tm, d),
        out_shape=jax.ShapeDtypeStruct((s, d), F32),
        scratch_shapes=[pltpu.VMEM((tm, d), BF16), pltpu.VMEM((tm, d_ff), BF16)],
        compiler_params=_params(),
        name="ffn",
    )(x, vec, w1, w3, w2)


def _out_kernel(y_ref, x_ref, vec_ref, w_ref, o_ref):
    y = _dot(y_ref[...], w_ref[...])
    o_ref[...] = _postnorm_residual(x_ref[...], y, vec_ref[...], 1.0)


def _out_proj(y, x, vec, w, layer):
    s, d = x.shape
    k = y.shape[1]
    tm = min(ROW_TILE, s)
    return pl.pallas_call(
        _out_kernel,
        grid=(s // tm,),
        in_specs=[_row_spec(tm, k), _row_spec(tm, d), _const_spec((8, d)), _layer_spec(layer, k, d, 0)],
        out_specs=_row_spec(tm, d),
        out_shape=jax.ShapeDtypeStruct((s, d), F32),
        compiler_params=_params(),
        name="out_proj",
    )(y, x, vec, w)


def _ml_in_kernel(x_ref, vec_ref, wq_ref, wv_ref, wo_ref, wkt_ref, wgt_ref, br_ref,
                  voq_ref, kt_ref, grow_ref, *, k_scale):
    h = _prenorm_mod(x_ref[...], vec_ref[...]).astype(BF16)
    base = 0
    for w_ref, act in ((wv_ref, None), (wo_ref, _sigmoid), (wq_ref, None)):
        for j in range(w_ref.shape[1] // 512):
            r = _dot(h, w_ref[:, j * 512:(j + 1) * 512])
            voq_ref[:, base:base + 512] = (r if act is None else act(r)).astype(BF16)
            base += 512
    kt_ref[...] = (_dot_nt(wkt_ref[...], h) * k_scale).astype(BF16)
    grow_ref[...] = _dot_nt(wgt_ref[...], h) + br_ref[...]


def _lane_tile(a, n):
    return jnp.concatenate([a] * n, axis=1)


def _ml_core_kernel(q_ref, kt_ref, v_ref, o_ref, grow_ref, nw_ref, triu_ref, out_ref,
                    c_ref, m_ref, *, heads, dk, dv):
    L = CHUNK
    assert L == LANES

    @pl.when(pl.program_id(0) == 0)
    def _():
        c_ref[...] = jnp.zeros_like(c_ref)
        m_ref[...] = jnp.zeros_like(m_ref)

    row = lax.broadcasted_iota(jnp.int32, (L, L), 0)
    col = lax.broadcasted_iota(jnp.int32, (L, L), 1)
    causal = col <= row

    gr = grow_ref[...]
    sub = lax.broadcasted_iota(jnp.int32, gr.shape, 0)
    lf_all = jnp.where((sub >= heads) & (sub < 2 * heads), _log_sigmoid(gr), 0.0)
    b_all = _dot(jnp.concatenate(_split_bf16(lf_all, 3), axis=1), triu_ref[...])
    ones_blk = jnp.ones((L, LANES), BF16)

    for h in range(heads):
        q = q_ref[:, h * dk:(h + 1) * dk]
        kt = kt_ref[h * dk:(h + 1) * dk, :]
        v_aug = jnp.concatenate([v_ref[:, h * dv:(h + 1) * dv], ones_blk], axis=1)
        lf_r = lf_all[heads + h:heads + h + 1, :]
        r = gr[h:h + 1, :] - b_all[heads + h:heads + h + 1, :]
        m_prev = m_ref[h:h + 1, :]

        b_c = jnp.sum(jnp.where(causal, lf_r, 0.0), axis=1, keepdims=True)
        r_mask = jnp.where(causal, r, -jnp.inf)
        g = jnp.maximum(jnp.max(r_mask, axis=1, keepdims=True), m_prev)
        s = _dot(q, kt) * jnp.exp(r_mask - g)
        inter = jnp.exp(m_prev - g)
        c_old = c_ref[h]
        numden = _dot(s.astype(BF16), v_aug) + _lane_tile(inter, dv // LANES + 1) * _dot(q, c_old.astype(BF16))
        den = jnp.maximum(jnp.abs(numden[:, dv:]), jnp.exp(-(b_c + g)))
        hh = numden[:, :dv] / _lane_tile(den, dv // LANES)

        g_last = g[L - 1:L, :]
        w_r = jnp.exp(r - g_last)
        decay = jnp.exp(m_prev - g_last)
        ktw = (kt.astype(F32) * w_r).astype(BF16)
        c_ref[h] = _lane_tile(decay, dv // LANES + 1) * c_old + _dot(ktw, v_aug)
        m_ref[h:h + 1, :] = jnp.sum(lf_r, axis=1, keepdims=True) + g_last

        hn = hh * lax.rsqrt(jnp.mean(hh * hh, axis=-1, keepdims=True) + EPS)
        gate = o_ref[:, h * dv:(h + 1) * dv].astype(F32)
        out_ref[:, h * dv:(h + 1) * dv] = (hn * nw_ref[:, h * dv:(h + 1) * dv] * gate).astype(BF16)


def _mlstm_sublayer(x, vec, w_in16, w_in, layer, b_gate, norm_w, w_out16):
    s, d = x.shape
    heads = ML_HEADS
    qk = (w_in.shape[1] - 2 * heads) // 6
    dvt = 2 * qk
    dk, dv = qk // heads, dvt // heads
    tm = min(ROW_TILE, s)

    wkt = w_in[:, qk:2 * qk].T.astype(BF16)
    wgate = w_in[:, 2 * qk + 2 * dvt:]
    wgt = jnp.pad(wgate.T, ((0, 16 - 2 * heads), (0, 0))).astype(BF16)
    br = jnp.pad(b_gate, (0, 16 - 2 * heads)).reshape(16, 1)
    n_main = qk + 2 * dvt

    qvo, kt, grow = pl.pallas_call(
        functools.partial(_ml_in_kernel, k_scale=dk ** -0.5),
        grid=(s // tm,),
        in_specs=[_row_spec(tm, d), _const_spec((8, d)),
                  _layer_spec(layer, d, qk, 0), _layer_spec(layer, d, dvt, 1), _layer_spec(layer, d, dvt, 2),
                  _const_spec((qk, d)), _const_spec((16, d)), _const_spec((16, 1))],
        out_specs=[_row_spec(tm, n_main), _col_spec(qk, tm), _col_spec(16, tm)],
        out_shape=[jax.ShapeDtypeStruct((s, n_main), BF16), jax.ShapeDtypeStruct((qk, s), BF16),
                   jax.ShapeDtypeStruct((16, s), F32)],
        compiler_params=_params(),
        name="mlstm_in",
    )(x, vec, w_in16, w_in16, w_in16, wkt, wgt, br)

    L = CHUNK
    tri = jnp.arange(L)[None, :] >= jnp.arange(L)[:, None]
    triu3 = jnp.tile(tri, (3, 1)).astype(BF16)
    hs = pl.pallas_call(
        functools.partial(_ml_core_kernel, heads=heads, dk=dk, dv=dv),
        grid=(s // L,),
        in_specs=[pl.BlockSpec((L, qk), lambda i: (i, 2 * dvt // qk)),
                  _col_spec(qk, L),
                  pl.BlockSpec((L, dvt), lambda i: (i, 0)),
                  pl.BlockSpec((L, dvt), lambda i: (i, 1)),
                  _col_spec(16, L), _const_spec((1, dvt)), _const_spec((3 * L, L))],
        out_specs=_row_spec(L, dvt),
        out_shape=jax.ShapeDtypeStruct((s, dvt), BF16),
        scratch_shapes=[pltpu.VMEM((heads, dk, dv + LANES), F32), pltpu.VMEM((8, LANES), F32)],
        compiler_params=_params(),
        name="mlstm_core",
    )(qvo, kt, qvo, qvo, grow, norm_w.reshape(1, dvt), triu3)
    return _out_proj(hs, x, vec, w_out16, layer)


def _mb_in_kernel(x_ref, vec_ref, wzx_ref, wd_ref, wdt_ref, z_ref, xbc_ref, dtc_ref, dtr_ref):
    h = _prenorm_mod(x_ref[...], vec_ref[...]).astype(BF16)
    di = z_ref.shape[1]
    for j in range(di // 512):
        cols = slice(j * 512, (j + 1) * 512)
        z_ref[:, cols] = _silu(_dot(h, wzx_ref[:, cols])).astype(BF16)
    for j in range(xbc_ref.shape[1] // 512):
        cols = slice(j * 512, (j + 1) * 512)
        xbc_ref[:, cols] = _dot(h, wzx_ref[:, di + j * 512:di + (j + 1) * 512]).astype(BF16)
    dtc_ref[...] = _dot(h, wd_ref[...])
    dtr_ref[...] = _dot_nt(wdt_ref[...], h)


def _split_bf16(v, terms):
    parts = []
    for _ in range(terms):
        part = v.astype(BF16)
        parts.append(part)
        v = v - part.astype(F32)
    return parts


def _mb_core_kernel(z_ref, xbc_ref, dtc_ref, dtr_ref, cw_ref, cb_ref, dbc_ref, dbr_ref, ac_ref, ar_ref,
                    e_ref, dsk_ref, nw_ref, shift_ref, tril_ref, triu_ref, y_ref, xext_ref, xa_ref, st_ref,
                    *, di, groups, nstate, hpg, p):
    Q = CHUNK
    gw = hpg * p
    conv_dim = di + 2 * groups * nstate

    @pl.when(pl.program_id(0) == 0)
    def _():
        xext_ref[Q:2 * Q, :] = jnp.zeros((Q, conv_dim), BF16)
        st_ref[...] = jnp.zeros_like(st_ref)

    xext_ref[0:Q, :] = xext_ref[Q:2 * Q, :]
    xext_ref[Q:2 * Q, :] = xbc_ref[...]
    for j in range(conv_dim // 512):
        cols = slice(j * 512, (j + 1) * 512)
        shifted = _dot(shift_ref[...], xext_ref[:, cols])
        acc = cb_ref[:, cols] + cw_ref[MB_CONV - 1:MB_CONV, cols] * xbc_ref[:, cols].astype(F32)
        for k in range(MB_CONV - 1):
            acc = acc + cw_ref[k:k + 1, cols] * shifted[k * Q:(k + 1) * Q]
        xa_ref[:, cols] = _silu(acc)

    row = lax.broadcasted_iota(jnp.int32, (Q, Q), 0)
    col = lax.broadcasted_iota(jnp.int32, (Q, Q), 1)
    causal = col <= row

    dt_c = _softplus(dtc_ref[...] + dbc_ref[...])
    dt_r = _softplus(dtr_ref[...] + dbr_ref[...])
    da_c = jnp.concatenate(_split_bf16(dt_c * -jnp.exp(ac_ref[...]), 3), axis=0)
    da_r = jnp.concatenate(_split_bf16(dt_r * -jnp.exp(ar_ref[...]), 3), axis=1)
    a_c = _dot(tril_ref[...], da_c)
    a_r = _dot(da_r, triu_ref[...])
    a_last = a_c[Q - 1:Q, :]
    per_head = jnp.concatenate([dt_c, jnp.exp(a_c), jnp.exp(a_last - a_c),
                                jnp.broadcast_to(jnp.exp(a_last), (16, LANES))], axis=0)
    per_head = jnp.concatenate(_split_bf16(per_head, 2), axis=1)
    lane = lax.broadcasted_iota(jnp.int32, (Q, LANES), 1)

    for g in range(groups):
        gcols = slice(g * gw, (g + 1) * gw)
        per_chan = _dot(per_head, e_ref[:, gcols])
        dt_x, ea_x, dec_x = per_chan[0:Q], per_chan[Q:2 * Q], per_chan[2 * Q:3 * Q]
        eal = per_chan[3 * Q:3 * Q + 1]
        xs = xa_ref[:, gcols]
        bm = xa_ref[:, di + g * nstate:di + (g + 1) * nstate]
        cm = xa_ref[:, di + groups * nstate + g * nstate:di + groups * nstate + (g + 1) * nstate]
        cm16 = cm.astype(BF16)
        xdt = xs * dt_x
        xdt16 = xdt.astype(BF16)
        cb = jnp.where(causal, _dot_nt(cm16, bm.astype(BF16)), 0.0)

        pairs = []
        for hp in range(hpg // 2):
            x_pair = xdt16[:, hp * LANES:(hp + 1) * LANES]
            ys = []
            for e in range(2):
                hd = g * hpg + hp * 2 + e
                seg = a_c[:, hd:hd + 1] - a_r[hd:hd + 1, :]
                lmat = (cb * jnp.exp(jnp.minimum(seg, 0.0))).astype(BF16)
                ys.append(_dot(lmat, x_pair))
            pairs.append(jnp.where(lane < p, ys[0], ys[1]))
        y = jnp.concatenate(pairs, axis=1)

        st = st_ref[g]
        y = y + _dot(cm16, st.astype(BF16)) * ea_x
        y = y + dsk_ref[:, gcols] * xs

        xdec = (xdt * dec_x).astype(BF16)
        st_ref[g] = eal * st + _dot(bm.T.astype(BF16), xdec)

        y = y * z_ref[:, gcols].astype(F32)
        y = y * lax.rsqrt(jnp.mean(y * y, axis=-1, keepdims=True) + EPS)
        y_ref[:, gcols] = (y * nw_ref[:, gcols]).astype(BF16)


def _mamba_sublayer(x, vec, w_in16, w_in, layer, conv_w, conv_b, dt_bias, a_log, d_skip, norm_w, w_out16):
    s, d = x.shape
    heads = dt_bias.shape[0]
    p, groups, nstate = MB_HEADDIM, MB_GROUPS, MB_STATE
    hpg = heads // groups
    di = heads * p
    conv_dim = di + 2 * groups * nstate
    tm = min(ROW_TILE, s)

    wdt = w_in[:, di + conv_dim:]
    wd = jnp.pad(wdt, ((0, 0), (0, LANES - heads))).astype(BF16)
    wdtt = wdt.T.astype(BF16)

    z, xbc, dtc, dtr = pl.pallas_call(
        _mb_in_kernel,
        grid=(s // tm,),
        in_specs=[_row_spec(tm, d), _const_spec((8, d)), _layer_spec(layer, d, di + conv_dim, 0),
                  _const_spec((d, LANES)), _const_spec((heads, d))],
        out_specs=[_row_spec(tm, di), _row_spec(tm, conv_dim), _row_spec(tm, LANES), _col_spec(heads, tm)],
        out_shape=[jax.ShapeDtypeStruct((s, di), BF16), jax.ShapeDtypeStruct((s, conv_dim), BF16),
                   jax.ShapeDtypeStruct((s, LANES), F32), jax.ShapeDtypeStruct((heads, s), F32)],
        compiler_params=_params(),
        name="mamba_in",
    )(x, vec, w_in16, wd, wdtt)

    pad = (0, LANES - heads)
    Q = CHUNK
    head_of_chan = jnp.arange(di)[None, :] // p
    expand = jnp.tile(jnp.arange(LANES)[:, None] == head_of_chan, (2, 1)).astype(BF16)
    t_out = jnp.arange((MB_CONV - 1) * Q)
    src = Q + t_out % Q - (MB_CONV - 1) + t_out // Q
    shift = (jnp.arange(2 * Q)[None, :] == src[:, None]).astype(BF16)
    tri = jnp.arange(Q)[None, :] <= jnp.arange(Q)[:, None]
    tril3 = jnp.tile(tri, (1, 3)).astype(BF16)
    triu3 = jnp.tile(tri.T, (3, 1)).astype(BF16)
    y = pl.pallas_call(
        functools.partial(_mb_core_kernel, di=di, groups=groups, nstate=nstate, hpg=hpg, p=p),
        grid=(s // Q,),
        in_specs=[_row_spec(Q, di), _row_spec(Q, conv_dim), _row_spec(Q, LANES), _col_spec(heads, Q),
                  _const_spec((MB_CONV, conv_dim)), _const_spec((1, conv_dim)),
                  _const_spec((1, LANES)), _const_spec((heads, 1)),
                  _const_spec((1, LANES)), _const_spec((heads, 1)),
                  _const_spec((2 * LANES, di)), _const_spec((1, di)), _const_spec((1, di)),
                  _const_spec(((MB_CONV - 1) * Q, 2 * Q)), _const_spec((Q, 3 * Q)), _const_spec((3 * Q, Q))],
        out_specs=_row_spec(Q, di),
        out_shape=jax.ShapeDtypeStruct((s, di), BF16),
        scratch_shapes=[pltpu.VMEM((2 * Q, conv_dim), BF16), pltpu.VMEM((Q, conv_dim), F32),
                        pltpu.VMEM((groups, nstate, hpg * p), F32)],
        compiler_params=_params(),
        name="mamba_core",
    )(z, xbc, dtc, dtr, conv_w, conv_b.reshape(1, conv_dim),
      jnp.pad(dt_bias, pad).reshape(1, LANES), dt_bias.reshape(heads, 1),
      jnp.pad(a_log, pad).reshape(1, LANES), a_log.reshape(heads, 1),
      expand, jnp.repeat(d_skip, p).reshape(1, di), norm_w.reshape(1, di), shift, tril3, triu3)
    return _out_proj(y, x, vec, w_out16, layer)


def kernel(x, c, ada_w, ada_b, norm_pre, norm_post, ffn_w1, ffn_w3, ffn_w2, ml_w_in, ml_b_gate, ml_norm_w,
           ml_w_out, mb_w_in, mb_conv_w, mb_conv_b, mb_dt_bias, mb_A_log, mb_D, mb_norm_w, mb_w_out):
    batch, seq, d = x.shape
    assert batch == 1, "the kernels assume one sequence"
    depth = ada_w.shape[0]
    mod = _ada_mod(c, ada_w, ada_b).reshape(depth, N_SUB, 3, d)
    pad_rows = jnp.zeros((3, d), F32)
    d_ff = ffn_w1.shape[-1]
    w1 = ffn_w1.astype(BF16).reshape(depth * 2, d, d_ff)
    w3 = ffn_w3.astype(BF16).reshape(depth * 2, d, d_ff)
    w2 = ffn_w2.astype(BF16).reshape(depth * 2, d_ff, d)
    ml_in16, ml_out16 = ml_w_in.astype(BF16), ml_w_out.astype(BF16)
    mb_in16, mb_out16 = mb_w_in.astype(BF16), mb_w_out.astype(BF16)
    h = x.reshape(seq, d)
    for i in range(depth):
        j = i // 2
        for sub in range(N_SUB):
            vec = jnp.concatenate([norm_pre[i, sub][None], mod[i, sub, 0][None], mod[i, sub, 1][None],
                                   norm_post[i, sub][None], mod[i, sub, 2][None], pad_rows], axis=0)
            if sub != 1:
                h = _ffn_sublayer(h, vec, w1, w3, w2, 2 * i + sub // 2, FFN_HALF)
            elif i % 2 == 0:
                h = _mlstm_sublayer(h, vec, ml_in16, ml_w_in[j], j, ml_b_gate[j], ml_norm_w[j], ml_out16)
            else:
                h = _mamba_sublayer(h, vec, mb_in16, mb_w_in[j], j, mb_conv_w[j], mb_conv_b[j], mb_dt_bias[j],
                                    mb_A_log[j], mb_D[j], mb_norm_w[j], mb_out16)
    return h.reshape(batch, seq, d)
```

```python
import functools

import jax
import jax.numpy as jnp
from jax import lax
from jax.experimental import pallas as pl
from jax.experimental.pallas import tpu as pltpu

F32 = jnp.float32
BF16 = jnp.bfloat16
HIGHEST = lax.Precision.HIGHEST
EPS = 1e-6
LOG2_E = 1.4426950408889634

N_SUB = 3
FFN_HALF = 0.5
ML_HEADS = 4
MB_HEADDIM = 64
MB_GROUPS = 4
MB_STATE = 128
MB_CONV = 4

LANES = 128
MXU_N = 256
ROW_TILE = 512
CHUNK = 128
ML_CHUNKS_PER_STEP = 2
VMEM_LIMIT = 56 * 1024 * 1024

_NT = (((1,), (1,)), ((), ()))


def _dot(a, b, precision=None):
    return jnp.dot(a, b, preferred_element_type=F32, precision=precision)


def _dot_nt(a, b):
    return lax.dot_general(a, b, _NT, preferred_element_type=F32)


def _sigmoid(v):
    return 0.5 + 0.5 * jnp.tanh(0.5 * v)


def _silu(v):
    half = 0.5 * v
    return half + half * jnp.tanh(half)


def _softplus(v):
    return jnp.maximum(v, 0.0) + jnp.log(1.0 + jnp.exp(-jnp.abs(v)))


def _log_sigmoid(v):
    return -_softplus(-v)


def _prenorm_mod(x, vec):
    xn = x * lax.rsqrt(jnp.mean(x * x, axis=-1, keepdims=True) + EPS)
    return (xn * vec[0:1]) * (1.0 + vec[2:3]) + vec[1:2]


def _postnorm_residual(x, y, vec, weight):
    yn = y * lax.rsqrt(jnp.mean(y * y, axis=-1, keepdims=True) + EPS) * vec[3:4]
    return x + (weight * vec[4:5]) * yn


def _const_spec(shape):
    zeros = (0,) * len(shape)
    return pl.BlockSpec(shape, lambda i: zeros, pipeline_mode=pl.Buffered(1))


def _layer_spec(layer, rows, cols, col_block):
    return pl.BlockSpec((None, rows, cols), lambda i: (layer, 0, col_block), pipeline_mode=pl.Buffered(1))


def _row_spec(rows, cols):
    return pl.BlockSpec((rows, cols), lambda i: (i, 0))


def _col_spec(rows, cols):
    return pl.BlockSpec((rows, cols), lambda i: (0, i))


def _params():
    return pltpu.CompilerParams(dimension_semantics=("arbitrary",), vmem_limit_bytes=VMEM_LIMIT)


def _ada_kernel(c_ref, w_ref, b_ref, o_ref):
    o_ref[0] = jnp.sum(_silu(c_ref[...]) * w_ref[0], axis=0, keepdims=True) + b_ref[0]


def _ada_mod(c, ada_w, ada_b):
    depth, d, n = ada_w.shape
    tn = n // 6
    out = pl.pallas_call(
        _ada_kernel,
        grid=(depth, n // tn),
        in_specs=[pl.BlockSpec((d, 1), lambda i, j: (0, 0)),
                  pl.BlockSpec((1, d, tn), lambda i, j: (i, 0, j)),
                  pl.BlockSpec((1, 1, tn), lambda i, j: (i, 0, j))],
        out_specs=pl.BlockSpec((1, 1, tn), lambda i, j: (i, 0, j)),
        out_shape=jax.ShapeDtypeStruct((depth, 1, n), F32),
        compiler_params=pltpu.CompilerParams(dimension_semantics=("arbitrary", "arbitrary"),
                                             vmem_limit_bytes=VMEM_LIMIT),
        name="ada_mod",
    )(c.reshape(d, 1), ada_w, ada_b.reshape(depth, 1, n))
    return out[:, 0, :]


def _ffn_kernel(x_ref, vec_ref, w1_ref, w3_ref, w2_ref, o_ref, h_ref, act_ref, *, weight, d_ff):
    vec = vec_ref[...]
    half = x_ref.shape[0] // 2
    for p in range(2):
        rows = slice(p * half, (p + 1) * half)
        h_ref[rows, :] = _prenorm_mod(x_ref[rows, :], vec).astype(BF16)
        for j in range(d_ff // MXU_N):
            cols = slice(j * MXU_N, (j + 1) * MXU_N)
            h = h_ref[rows, :]
            a = _dot(h, w1_ref[:, cols])
            b = _dot(h, w3_ref[:, cols])
            act_ref[rows, cols] = (_silu(a) * b).astype(BF16)
        y = _dot(act_ref[rows, :], w2_ref[...])
        o_ref[rows, :] = _postnorm_residual(x_ref[rows, :], y, vec, weight)


def _ffn_sublayer(x, vec, w1, w3, w2, layer, weight):
    s, d = x.shape
    d_ff = w1.shape[2]
    tm = min(ROW_TILE, s)
    return pl.pallas_call(
        functools.partial(_ffn_kernel, weight=weight, d_ff=d_ff),
        grid=(s // tm,),
        in_specs=[_row_spec(tm, d), _const_spec((8, d)), _layer_spec(layer, d, d_ff, 0),
                  _layer_spec(layer, d, d_ff, 0), _layer_spec(layer, d_ff, d, 0)],
        out_specs=_row_spec(tm, d),
        out_shape=jax.ShapeDtypeStruct((s, d), F32),
        scratch_shapes=[pltpu.VMEM((tm, d), BF16), pltpu.VMEM((tm, d_ff), BF16)],
        compiler_params=_params(),
        name="ffn",
    )(x, vec, w1, w3, w2)


def _halves(ref):
    half = ref.shape[0] // 2
    return (slice(0, half), slice(half, 2 * half))


def _out_kernel(y_ref, x_ref, vec_ref, w_ref, o_ref):
    for rows in _halves(x_ref):
        y = _dot(y_ref[rows, :], w_ref[...])
        o_ref[rows, :] = _postnorm_residual(x_ref[rows, :], y, vec_ref[...], 1.0)


def _out_proj(y, x, vec, w, layer):
    s, d = x.shape
    k = y.shape[1]
    tm = min(ROW_TILE, s)
    return pl.pallas_call(
        _out_kernel,
        grid=(s // tm,),
        in_specs=[_row_spec(tm, k), _row_spec(tm, d), _const_spec((8, d)), _layer_spec(layer, k, d, 0)],
        out_specs=_row_spec(tm, d),
        out_shape=jax.ShapeDtypeStruct((s, d), F32),
        compiler_params=_params(),
        name="out_proj",
    )(y, x, vec, w)


def _ml_in_kernel(x_ref, vec_ref, wq_ref, wv_ref, wo_ref, wkt_ref, wgt_ref, br_ref,
                  voq_ref, kt_ref, grow_ref, *, k_scale):
    for rows in _halves(x_ref):
        h = _prenorm_mod(x_ref[rows, :], vec_ref[...]).astype(BF16)
        base = 0
        for w_ref, act in ((wv_ref, None), (wo_ref, _sigmoid), (wq_ref, None)):
            for j in range(w_ref.shape[1] // 512):
                r = _dot(h, w_ref[:, j * 512:(j + 1) * 512])
                voq_ref[rows, base:base + 512] = (r if act is None else act(r)).astype(BF16)
                base += 512
        kt_ref[:, rows] = (_dot_nt(wkt_ref[...], h) * k_scale).astype(BF16)
        grow_ref[:, rows] = _dot_nt(wgt_ref[...], h) + br_ref[...]


def _lane_tile(a, n):
    return jnp.concatenate([a] * n, axis=1)


def _ml_core_kernel(q_ref, kt_ref, v_ref, o_ref, grow_ref, nw_ref, triu_ref, out_ref,
                    c_ref, m_ref, *, heads, dk, dv):
    L = CHUNK
    assert L == LANES

    @pl.when(pl.program_id(0) == 0)
    def _():
        c_ref[...] = jnp.zeros_like(c_ref)
        m_ref[...] = jnp.zeros_like(m_ref)

    row = lax.broadcasted_iota(jnp.int32, (L, L), 0)
    col = lax.broadcasted_iota(jnp.int32, (L, L), 1)
    causal = col <= row

    ones_blk = jnp.ones((L, LANES), BF16)
    sub = lax.broadcasted_iota(jnp.int32, (grow_ref.shape[0], L), 0)
    is_fg = (sub >= heads) & (sub < 2 * heads)

    for c in range(q_ref.shape[0] // L):
        t = slice(c * L, (c + 1) * L)
        gr = grow_ref[:, t]
        lf_all = jnp.where(is_fg, _log_sigmoid(gr), 0.0)
        b_all = _dot(jnp.concatenate(_split_bf16(lf_all, 3), axis=1), triu_ref[...])

        for h in range(heads):
            hv = slice(h * dv, (h + 1) * dv)
            q = q_ref[t, h * dk:(h + 1) * dk]
            kt = kt_ref[h * dk:(h + 1) * dk, t]
            v_aug = jnp.concatenate([v_ref[t, hv], ones_blk], axis=1)
            lf_r = lf_all[heads + h:heads + h + 1, :]
            r = gr[h:h + 1, :] - b_all[heads + h:heads + h + 1, :]
            m_prev = m_ref[h:h + 1, :]

            b_c = jnp.sum(jnp.where(causal, lf_r, 0.0), axis=1, keepdims=True)
            r_mask = jnp.where(causal, r, -jnp.inf)
            g = jnp.maximum(jnp.max(r_mask, axis=1, keepdims=True), m_prev)
            s = _dot(q, kt) * jnp.exp(r_mask - g)
            inter = jnp.exp(m_prev - g)
            c_old = c_ref[h]
            numden = (_dot(s.astype(BF16), v_aug)
                      + _lane_tile(inter, dv // LANES + 1) * _dot(q, c_old.astype(BF16)))
            den = jnp.maximum(jnp.abs(numden[:, dv:]), jnp.exp(-(b_c + g)))
            hh = numden[:, :dv] * _lane_tile(1.0 / den, dv // LANES)

            g_last = g[L - 1:L, :]
            w_r = jnp.exp(r - g_last)
            decay = jnp.exp(m_prev - g_last)
            ktw = (kt.astype(F32) * w_r).astype(BF16)
            c_ref[h] = _lane_tile(decay, dv // LANES + 1) * c_old + _dot(ktw, v_aug)
            m_ref[h:h + 1, :] = jnp.sum(lf_r, axis=1, keepdims=True) + g_last

            hn = hh * lax.rsqrt(jnp.mean(hh * hh, axis=-1, keepdims=True) + EPS)
            gate = o_ref[t, hv].astype(F32)
            out_ref[t, hv] = (hn * nw_ref[:, hv] * gate).astype(BF16)


def _mlstm_sublayer(x, vec, w_in16, w_in, layer, b_gate, norm_w, w_out16):
    s, d = x.shape
    heads = ML_HEADS
    qk = (w_in.shape[1] - 2 * heads) // 6
    dvt = 2 * qk
    dk, dv = qk // heads, dvt // heads
    tm = min(ROW_TILE, s)

    wkt = w_in[:, qk:2 * qk].T.astype(BF16)
    wgate = w_in[:, 2 * qk + 2 * dvt:]
    wgt = jnp.pad(wgate.T, ((0, 16 - 2 * heads), (0, 0))).astype(BF16)
    br = jnp.pad(b_gate, (0, 16 - 2 * heads)).reshape(16, 1)
    n_main = qk + 2 * dvt

    qvo, kt, grow = pl.pallas_call(
        functools.partial(_ml_in_kernel, k_scale=dk ** -0.5),
        grid=(s // tm,),
        in_specs=[_row_spec(tm, d), _const_spec((8, d)),
                  _layer_spec(layer, d, qk, 0), _layer_spec(layer, d, dvt, 1), _layer_spec(layer, d, dvt, 2),
                  _const_spec((qk, d)), _const_spec((16, d)), _const_spec((16, 1))],
        out_specs=[_row_spec(tm, n_main), _col_spec(qk, tm), _col_spec(16, tm)],
        out_shape=[jax.ShapeDtypeStruct((s, n_main), BF16), jax.ShapeDtypeStruct((qk, s), BF16),
                   jax.ShapeDtypeStruct((16, s), F32)],
        compiler_params=_params(),
        name="mlstm_in",
    )(x, vec, w_in16, w_in16, w_in16, wkt, wgt, br)

    L = CHUNK
    tri = jnp.arange(L)[None, :] >= jnp.arange(L)[:, None]
    triu3 = jnp.tile(tri, (3, 1)).astype(BF16)
    blk = min(ML_CHUNKS_PER_STEP * L, s)
    hs = pl.pallas_call(
        functools.partial(_ml_core_kernel, heads=heads, dk=dk, dv=dv),
        grid=(s // blk,),
        in_specs=[pl.BlockSpec((blk, qk), lambda i: (i, 2 * dvt // qk)),
                  _col_spec(qk, blk),
                  pl.BlockSpec((blk, dvt), lambda i: (i, 0)),
                  pl.BlockSpec((blk, dvt), lambda i: (i, 1)),
                  _col_spec(16, blk), _const_spec((1, dvt)), _const_spec((3 * L, L))],
        out_specs=_row_spec(blk, dvt),
        out_shape=jax.ShapeDtypeStruct((s, dvt), BF16),
        scratch_shapes=[pltpu.VMEM((heads, dk, dv + LANES), F32), pltpu.VMEM((8, LANES), F32)],
        compiler_params=_params(),
        name="mlstm_core",
    )(qvo, kt, qvo, qvo, grow, norm_w.reshape(1, dvt), triu3)
    return _out_proj(hs, x, vec, w_out16, layer)


def _mb_in_kernel(x_ref, vec_ref, wzx_ref, wd_ref, wdt_ref, z_ref, xbc_ref, dtc_ref, dtr_ref):
    di = z_ref.shape[1]
    for rows in _halves(x_ref):
        h = _prenorm_mod(x_ref[rows, :], vec_ref[...]).astype(BF16)
        for j in range(di // 512):
            cols = slice(j * 512, (j + 1) * 512)
            z_ref[rows, cols] = _silu(_dot(h, wzx_ref[:, cols])).astype(BF16)
        for j in range(xbc_ref.shape[1] // 512):
            cols = slice(j * 512, (j + 1) * 512)
            xbc_ref[rows, cols] = _dot(h, wzx_ref[:, di + j * 512:di + (j + 1) * 512]).astype(BF16)
        dtc_ref[rows, :] = _dot(h, wd_ref[...])
        dtr_ref[:, rows] = _dot_nt(wdt_ref[...], h)


def _split_bf16(v, terms):
    parts = []
    for _ in range(terms):
        part = v.astype(BF16)
        parts.append(part)
        v = v - part.astype(F32)
    return parts


def _mb_core_kernel(z_ref, xbc_ref, dtc_ref, dtr_ref, cw_ref, cb_ref, dbc_ref, dbr_ref, ac_ref, ar_ref,
                    e_ref, dsk_ref, nw_ref, shift_ref, tril_ref, triu_ref, y_ref, xext_ref, xa_ref, st_ref,
                    *, di, groups, nstate, hpg, p):
    Q = CHUNK
    gw = hpg * p
    conv_dim = di + 2 * groups * nstate

    @pl.when(pl.program_id(0) == 0)
    def _():
        xext_ref[Q:2 * Q, :] = jnp.zeros((Q, conv_dim), BF16)
        st_ref[...] = jnp.zeros_like(st_ref)

    xext_ref[0:Q, :] = xext_ref[Q:2 * Q, :]
    xext_ref[Q:2 * Q, :] = xbc_ref[...]

    def conv_silu(first_col, width):
        cols = slice(first_col, first_col + width)
        shifted = _dot(shift_ref[...], xext_ref[:, cols])
        acc = cb_ref[:, cols] + cw_ref[MB_CONV - 1:MB_CONV, cols] * xbc_ref[:, cols].astype(F32)
        for k in range(MB_CONV - 1):
            acc = acc + cw_ref[k:k + 1, cols] * shifted[k * Q:(k + 1) * Q]
        xa_ref[:, cols] = _silu(acc)

    for first_col in range(di, conv_dim, gw):
        conv_silu(first_col, gw)
    conv_silu(0, gw)

    row = lax.broadcasted_iota(jnp.int32, (Q, Q), 0)
    col = lax.broadcasted_iota(jnp.int32, (Q, Q), 1)
    causal = col <= row

    dt_c = _softplus(dtc_ref[...] + dbc_ref[...])
    dt_r = _softplus(dtr_ref[...] + dbr_ref[...])
    da_c = jnp.concatenate(_split_bf16(dt_c * -jnp.exp(ac_ref[...]), 3), axis=0)
    da_r = jnp.concatenate(_split_bf16(dt_r * -jnp.exp(ar_ref[...]), 3), axis=1)
    a_c = _dot(tril_ref[...], da_c)
    a_r = _dot(da_r, triu_ref[...])
    a_last = a_c[Q - 1:Q, :]
    per_head = jnp.concatenate([dt_c, jnp.exp(a_c), jnp.exp(a_last - a_c),
                                jnp.broadcast_to(jnp.exp(a_last), (16, LANES))], axis=0)
    per_head = jnp.concatenate(_split_bf16(per_head, 2), axis=1)
    lane = lax.broadcasted_iota(jnp.int32, (Q, LANES), 1)
    a2_c, a2_r = a_c * LOG2_E, a_r * LOG2_E

    for g in range(groups):
        gcols = slice(g * gw, (g + 1) * gw)
        if g + 1 < groups:
            conv_silu((g + 1) * gw, gw)
        per_chan = _dot(per_head, e_ref[:, gcols])
        dt_x, ea_x, dec_x = per_chan[0:Q], per_chan[Q:2 * Q], per_chan[2 * Q:3 * Q]
        eal = per_chan[3 * Q:3 * Q + 1]
        xs = xa_ref[:, gcols]
        bm = xa_ref[:, di + g * nstate:di + (g + 1) * nstate]
        cm = xa_ref[:, di + groups * nstate + g * nstate:di + groups * nstate + (g + 1) * nstate]
        cm16 = cm.astype(BF16)
        xdt = xs * dt_x
        xdt16 = xdt.astype(BF16)
        cb = jnp.where(causal, _dot_nt(cm16, bm.astype(BF16)), 0.0)

        pairs = []
        for hp in range(hpg // 2):
            x_pair = xdt16[:, hp * LANES:(hp + 1) * LANES]
            ys = []
            for e in range(2):
                hd = g * hpg + hp * 2 + e
                seg = a2_c[:, hd:hd + 1] - a2_r[hd:hd + 1, :]
                lmat = (cb * jnp.exp2(jnp.minimum(seg, 0.0))).astype(BF16)
                ys.append(_dot(lmat, x_pair))
            pairs.append(jnp.where(lane < p, ys[0], ys[1]))
        y = jnp.concatenate(pairs, axis=1)

        st = st_ref[g]
        y = y + _dot(cm16, st.astype(BF16)) * ea_x
        y = y + dsk_ref[:, gcols] * xs

        xdec = (xdt * dec_x).astype(BF16)
        st_ref[g] = eal * st + _dot(bm.T.astype(BF16), xdec)

        y = y * z_ref[:, gcols].astype(F32)
        y = y * lax.rsqrt(jnp.mean(y * y, axis=-1, keepdims=True) + EPS)
        y_ref[:, gcols] = (y * nw_ref[:, gcols]).astype(BF16)


def _mamba_sublayer(x, vec, w_in16, w_in, layer, conv_w, conv_b, dt_bias, a_log, d_skip, norm_w, w_out16):
    s, d = x.shape
    heads = dt_bias.shape[0]
    p, groups, nstate = MB_HEADDIM, MB_GROUPS, MB_STATE
    hpg = heads // groups
    di = heads * p
    conv_dim = di + 2 * groups * nstate
    tm = min(ROW_TILE, s)

    wdt = w_in[:, di + conv_dim:]
    wd = jnp.pad(wdt, ((0, 0), (0, LANES - heads))).astype(BF16)
    wdtt = wdt.T.astype(BF16)

    z, xbc, dtc, dtr = pl.pallas_call(
        _mb_in_kernel,
        grid=(s // tm,),
        in_specs=[_row_spec(tm, d), _const_spec((8, d)), _layer_spec(layer, d, di + conv_dim, 0),
                  _const_spec((d, LANES)), _const_spec((heads, d))],
        out_specs=[_row_spec(tm, di), _row_spec(tm, conv_dim), _row_spec(tm, LANES), _col_spec(heads, tm)],
        out_shape=[jax.ShapeDtypeStruct((s, di), BF16), jax.ShapeDtypeStruct((s, conv_dim), BF16),
                   jax.ShapeDtypeStruct((s, LANES), F32), jax.ShapeDtypeStruct((heads, s), F32)],
        compiler_params=_params(),
        name="mamba_in",
    )(x, vec, w_in16, wd, wdtt)

    pad = (0, LANES - heads)
    Q = CHUNK
    head_of_chan = jnp.arange(di)[None, :] // p
    expand = jnp.tile(jnp.arange(LANES)[:, None] == head_of_chan, (2, 1)).astype(BF16)
    t_out = jnp.arange((MB_CONV - 1) * Q)
    src = Q + t_out % Q - (MB_CONV - 1) + t_out // Q
    shift = (jnp.arange(2 * Q)[None, :] == src[:, None]).astype(BF16)
    tri = jnp.arange(Q)[None, :] <= jnp.arange(Q)[:, None]
    tril3 = jnp.tile(tri, (1, 3)).astype(BF16)
    triu3 = jnp.tile(tri.T, (3, 1)).astype(BF16)
    y = pl.pallas_call(
        functools.partial(_mb_core_kernel, di=di, groups=groups, nstate=nstate, hpg=hpg, p=p),
        grid=(s // Q,),
        in_specs=[_row_spec(Q, di), _row_spec(Q, conv_dim), _row_spec(Q, LANES), _col_spec(heads, Q),
                  _const_spec((MB_CONV, conv_dim)), _const_spec((1, conv_dim)),
                  _const_spec((1, LANES)), _const_spec((heads, 1)),
                  _const_spec((1, LANES)), _const_spec((heads, 1)),
                  _const_spec((2 * LANES, di)), _const_spec((1, di)), _const_spec((1, di)),
                  _const_spec(((MB_CONV - 1) * Q, 2 * Q)), _const_spec((Q, 3 * Q)), _const_spec((3 * Q, Q))],
        out_specs=_row_spec(Q, di),
        out_shape=jax.ShapeDtypeStruct((s, di), BF16),
        scratch_shapes=[pltpu.VMEM((2 * Q, conv_dim), BF16), pltpu.VMEM((Q, conv_dim), F32),
                        pltpu.VMEM((groups, nstate, hpg * p), F32)],
        compiler_params=_params(),
        name="mamba_core",
    )(z, xbc, dtc, dtr, conv_w, conv_b.reshape(1, conv_dim),
      jnp.pad(dt_bias, pad).reshape(1, LANES), dt_bias.reshape(heads, 1),
      jnp.pad(a_log, pad).reshape(1, LANES), a_log.reshape(heads, 1),
      expand, jnp.repeat(d_skip, p).reshape(1, di), norm_w.reshape(1, di), shift, tril3, triu3)
    return _out_proj(y, x, vec, w_out16, layer)


def kernel(x, c, ada_w, ada_b, norm_pre, norm_post, ffn_w1, ffn_w3, ffn_w2, ml_w_in, ml_b_gate, ml_norm_w,
           ml_w_out, mb_w_in, mb_conv_w, mb_conv_b, mb_dt_bias, mb_A_log, mb_D, mb_norm_w, mb_w_out):
    batch, seq, d = x.shape
    assert batch == 1, "the kernels assume one sequence"
    depth = ada_w.shape[0]
    mod = _ada_mod(c, ada_w, ada_b).reshape(depth, N_SUB, 3, d)
    pad_rows = jnp.zeros((3, d), F32)
    d_ff = ffn_w1.shape[-1]
    w1 = ffn_w1.astype(BF16).reshape(depth * 2, d, d_ff)
    w3 = ffn_w3.astype(BF16).reshape(depth * 2, d, d_ff)
    w2 = ffn_w2.astype(BF16).reshape(depth * 2, d_ff, d)
    ml_in16, ml_out16 = ml_w_in.astype(BF16), ml_w_out.astype(BF16)
    mb_in16, mb_out16 = mb_w_in.astype(BF16), mb_w_out.astype(BF16)
    h = x.reshape(seq, d)
    for i in range(depth):
        j = i // 2
        for sub in range(N_SUB):
            vec = jnp.concatenate([norm_pre[i, sub][None], mod[i, sub, 0][None], mod[i, sub, 1][None],
                                   norm_post[i, sub][None], mod[i, sub, 2][None], pad_rows], axis=0)
            if sub != 1:
                h = _ffn_sublayer(h, vec, w1, w3, w2, 2 * i + sub // 2, FFN_HALF)
            elif i % 2 == 0:
                h = _mlstm_sublayer(h, vec, ml_in16, ml_w_in[j], j, ml_b_gate[j], ml_norm_w[j], ml_out16)
            else:
                h = _mamba_sublayer(h, vec, mb_in16, mb_w_in[j], j, mb_conv_w[j], mb_conv_b[j], mb_dt_bias[j],
                                    mb_A_log[j], mb_D[j], mb_norm_w[j], mb_out16)
    return h.reshape(batch, seq, d)
```

```python
import functools

import jax
import jax.numpy as jnp
from jax import lax
from jax.experimental import pallas as pl
from jax.experimental.pallas import tpu as pltpu

F32 = jnp.float32
BF16 = jnp.bfloat16
HIGHEST = lax.Precision.HIGHEST
EPS = 1e-6
LOG2_E = 1.4426950408889634

N_SUB = 3
FFN_HALF = 0.5
ML_HEADS = 4
MB_HEADDIM = 64
MB_GROUPS = 4
MB_STATE = 128
MB_CONV = 4

LANES = 128
MXU_N = 256
ROW_TILE = 512
CHUNK = 128
ML_CHUNKS_PER_STEP = 2
MB_CHUNKS_PER_STEP = 2
VMEM_LIMIT = 56 * 1024 * 1024

_NT = (((1,), (1,)), ((), ()))


def _dot(a, b, precision=None):
    return jnp.dot(a, b, preferred_element_type=F32, precision=precision)


def _dot_nt(a, b):
    return lax.dot_general(a, b, _NT, preferred_element_type=F32)


def _sigmoid(v):
    return 0.5 + 0.5 * jnp.tanh(0.5 * v)


def _silu(v):
    half = 0.5 * v
    return half + half * jnp.tanh(half)


def _softplus(v):
    return jnp.maximum(v, 0.0) + jnp.log(1.0 + jnp.exp(-jnp.abs(v)))


def _log_sigmoid(v):
    return -_softplus(-v)


def _prenorm_mod(x, vec):
    xn = x * lax.rsqrt(jnp.mean(x * x, axis=-1, keepdims=True) + EPS)
    return (xn * vec[0:1]) * (1.0 + vec[2:3]) + vec[1:2]


def _postnorm_residual(x, y, vec, weight):
    yn = y * lax.rsqrt(jnp.mean(y * y, axis=-1, keepdims=True) + EPS) * vec[3:4]
    return x + (weight * vec[4:5]) * yn


def _const_spec(shape):
    zeros = (0,) * len(shape)
    return pl.BlockSpec(shape, lambda i: zeros, pipeline_mode=pl.Buffered(1))


def _layer_spec(layer, rows, cols, col_block):
    return pl.BlockSpec((None, rows, cols), lambda i: (layer, 0, col_block), pipeline_mode=pl.Buffered(1))


def _row_spec(rows, cols):
    return pl.BlockSpec((rows, cols), lambda i: (i, 0))


def _col_spec(rows, cols):
    return pl.BlockSpec((rows, cols), lambda i: (0, i))


def _params():
    return pltpu.CompilerParams(dimension_semantics=("arbitrary",), vmem_limit_bytes=VMEM_LIMIT)


def _ada_kernel(c_ref, w_ref, b_ref, o_ref):
    o_ref[0] = jnp.sum(_silu(c_ref[...]) * w_ref[0], axis=0, keepdims=True) + b_ref[0]


def _ada_mod(c, ada_w, ada_b):
    depth, d, n = ada_w.shape
    tn = n // 6
    out = pl.pallas_call(
        _ada_kernel,
        grid=(depth, n // tn),
        in_specs=[pl.BlockSpec((d, 1), lambda i, j: (0, 0)),
                  pl.BlockSpec((1, d, tn), lambda i, j: (i, 0, j)),
                  pl.BlockSpec((1, 1, tn), lambda i, j: (i, 0, j))],
        out_specs=pl.BlockSpec((1, 1, tn), lambda i, j: (i, 0, j)),
        out_shape=jax.ShapeDtypeStruct((depth, 1, n), F32),
        compiler_params=pltpu.CompilerParams(dimension_semantics=("arbitrary", "arbitrary"),
                                             vmem_limit_bytes=VMEM_LIMIT),
        name="ada_mod",
    )(c.reshape(d, 1), ada_w, ada_b.reshape(depth, 1, n))
    return out[:, 0, :]


def _ffn_kernel(x_ref, vec_ref, w1_ref, w3_ref, w2_ref, o_ref, h_ref, act_ref, *, weight, d_ff):
    vec = vec_ref[...]
    half = x_ref.shape[0] // 2
    for p in range(2):
        rows = slice(p * half, (p + 1) * half)
        h_ref[rows, :] = _prenorm_mod(x_ref[rows, :], vec).astype(BF16)
        for j in range(d_ff // MXU_N):
            cols = slice(j * MXU_N, (j + 1) * MXU_N)
            h = h_ref[rows, :]
            a = _dot(h, w1_ref[:, cols])
            b = _dot(h, w3_ref[:, cols])
            act_ref[rows, cols] = (_silu(a) * b).astype(BF16)
        y = _dot(act_ref[rows, :], w2_ref[...])
        o_ref[rows, :] = _postnorm_residual(x_ref[rows, :], y, vec, weight)


def _ffn_sublayer(x, vec, w1, w3, w2, layer, weight):
    s, d = x.shape
    d_ff = w1.shape[2]
    tm = min(ROW_TILE, s)
    return pl.pallas_call(
        functools.partial(_ffn_kernel, weight=weight, d_ff=d_ff),
        grid=(s // tm,),
        in_specs=[_row_spec(tm, d), _const_spec((8, d)), _layer_spec(layer, d, d_ff, 0),
                  _layer_spec(layer, d, d_ff, 0), _layer_spec(layer, d_ff, d, 0)],
        out_specs=_row_spec(tm, d),
        out_shape=jax.ShapeDtypeStruct((s, d), F32),
        scratch_shapes=[pltpu.VMEM((tm, d), BF16), pltpu.VMEM((tm, d_ff), BF16)],
        compiler_params=_params(),
        name="ffn",
    )(x, vec, w1, w3, w2)


def _halves(ref):
    half = ref.shape[0] // 2
    return (slice(0, half), slice(half, 2 * half))


def _ml_in_kernel(x_ref, vec_ref, wq_ref, wk_ref, wv_ref, wo_ref, wg_ref, bg_ref,
                  voq_ref, kt_ref, grow_ref, *, k_scale):
    for rows in _halves(x_ref):
        h = _prenorm_mod(x_ref[rows, :], vec_ref[...]).astype(BF16)
        base = 0
        for w_ref, act in ((wv_ref, None), (wo_ref, _sigmoid), (wq_ref, None)):
            for j in range(w_ref.shape[1] // 512):
                r = _dot(h, w_ref[:, j * 512:(j + 1) * 512])
                voq_ref[rows, base:base + 512] = (r if act is None else act(r)).astype(BF16)
                base += 512
        kt_ref[:, rows] = (_dot(h, wk_ref[...]) * k_scale).T.astype(BF16)
        grow_ref[:, rows] = (_dot(h, wg_ref[...]) + bg_ref[...]).T[0:grow_ref.shape[0], :]


def _lane_tile(a, n):
    return jnp.concatenate([a] * n, axis=1)


def _ml_core_kernel(q_ref, kt_ref, v_ref, o_ref, grow_ref, nw_ref, triu_ref, x_ref, vec_ref, wout_ref,
                    out_ref, c_ref, m_ref, hs_ref, *, heads, dk, dv):
    L = CHUNK
    assert L == LANES

    @pl.when(pl.program_id(0) == 0)
    def _():
        c_ref[...] = jnp.zeros_like(c_ref)
        m_ref[...] = jnp.zeros_like(m_ref)

    row = lax.broadcasted_iota(jnp.int32, (L, L), 0)
    col = lax.broadcasted_iota(jnp.int32, (L, L), 1)
    causal = col <= row

    ones_blk = jnp.ones((L, LANES), BF16)
    sub = lax.broadcasted_iota(jnp.int32, (grow_ref.shape[0], L), 0)
    is_fg = (sub >= heads) & (sub < 2 * heads)

    for c in range(q_ref.shape[0] // L):
        t = slice(c * L, (c + 1) * L)
        gr = grow_ref[:, t]
        lf_all = jnp.where(is_fg, _log_sigmoid(gr), 0.0)
        b_all = _dot(jnp.concatenate(_split_bf16(lf_all, 3), axis=1), triu_ref[...])

        for h in range(heads):
            hv = slice(h * dv, (h + 1) * dv)
            q = q_ref[t, h * dk:(h + 1) * dk]
            kt = kt_ref[h * dk:(h + 1) * dk, t]
            v_aug = jnp.concatenate([v_ref[t, hv], ones_blk], axis=1)
            lf_r = lf_all[heads + h:heads + h + 1, :]
            r = gr[h:h + 1, :] - b_all[heads + h:heads + h + 1, :]
            m_prev = m_ref[h:h + 1, :]

            b_c = jnp.sum(jnp.where(causal, lf_r, 0.0), axis=1, keepdims=True)
            r_mask = jnp.where(causal, r, -jnp.inf)
            g = jnp.maximum(jnp.max(r_mask, axis=1, keepdims=True), m_prev)
            s = _dot(q, kt) * jnp.exp(r_mask - g)
            inter = jnp.exp(m_prev - g)
            c_old = c_ref[h]
            numden = (_dot(s.astype(BF16), v_aug)
                      + _lane_tile(inter, dv // LANES + 1) * _dot(q, c_old.astype(BF16)))
            den = jnp.maximum(jnp.abs(numden[:, dv:]), jnp.exp(-(b_c + g)))
            hh = numden[:, :dv] * _lane_tile(1.0 / den, dv // LANES)

            g_last = g[L - 1:L, :]
            w_r = jnp.exp(r - g_last)
            decay = jnp.exp(m_prev - g_last)
            ktw = (kt.astype(F32) * w_r).astype(BF16)
            c_ref[h] = _lane_tile(decay, dv // LANES + 1) * c_old + _dot(ktw, v_aug)
            m_ref[h:h + 1, :] = jnp.sum(lf_r, axis=1, keepdims=True) + g_last

            hn = hh * lax.rsqrt(jnp.mean(hh * hh, axis=-1, keepdims=True) + EPS)
            gate = o_ref[t, hv].astype(F32)
            hs_ref[t, hv] = (hn * nw_ref[:, hv] * gate).astype(BF16)

        y = _dot(hs_ref[t, :], wout_ref[...])
        out_ref[t, :] = _postnorm_residual(x_ref[t, :], y, vec_ref[...], 1.0)


def _mlstm_sublayer(x, vec, w_in16, layer, b_gate, norm_w, w_out16):
    s, d = x.shape
    heads = ML_HEADS
    qk = (w_in16.shape[2] - 2 * heads) // 6
    dvt = 2 * qk
    dk, dv = qk // heads, dvt // heads
    tm = min(ROW_TILE, s)

    wg = jnp.pad(w_in16[layer, :, 2 * qk + 2 * dvt:], ((0, 0), (0, LANES - 2 * heads)))
    bg = jnp.pad(b_gate, (0, LANES - 2 * heads)).reshape(1, LANES)
    n_main = qk + 2 * dvt

    qvo, kt, grow = pl.pallas_call(
        functools.partial(_ml_in_kernel, k_scale=dk ** -0.5),
        grid=(s // tm,),
        in_specs=[_row_spec(tm, d), _const_spec((8, d)),
                  _layer_spec(layer, d, qk, 0), _layer_spec(layer, d, qk, 1),
                  _layer_spec(layer, d, dvt, 1), _layer_spec(layer, d, dvt, 2),
                  _const_spec((d, LANES)), _const_spec((1, LANES))],
        out_specs=[_row_spec(tm, n_main), _col_spec(qk, tm), _col_spec(16, tm)],
        out_shape=[jax.ShapeDtypeStruct((s, n_main), BF16), jax.ShapeDtypeStruct((qk, s), BF16),
                   jax.ShapeDtypeStruct((16, s), F32)],
        compiler_params=_params(),
        name="mlstm_in",
    )(x, vec, w_in16, w_in16, w_in16, w_in16, wg, bg)

    L = CHUNK
    tri = jnp.arange(L)[None, :] >= jnp.arange(L)[:, None]
    triu3 = jnp.tile(tri, (3, 1)).astype(BF16)
    blk = min(ML_CHUNKS_PER_STEP * L, s)
    return pl.pallas_call(
        functools.partial(_ml_core_kernel, heads=heads, dk=dk, dv=dv),
        grid=(s // blk,),
        in_specs=[pl.BlockSpec((blk, qk), lambda i: (i, 2 * dvt // qk)),
                  _col_spec(qk, blk),
                  pl.BlockSpec((blk, dvt), lambda i: (i, 0)),
                  pl.BlockSpec((blk, dvt), lambda i: (i, 1)),
                  _col_spec(16, blk), _const_spec((1, dvt)), _const_spec((3 * L, L)),
                  _row_spec(blk, d), _const_spec((8, d)), _layer_spec(layer, dvt, d, 0)],
        out_specs=_row_spec(blk, d),
        out_shape=jax.ShapeDtypeStruct((s, d), F32),
        scratch_shapes=[pltpu.VMEM((heads, dk, dv + LANES), F32), pltpu.VMEM((8, LANES), F32),
                        pltpu.VMEM((blk, dvt), BF16)],
        compiler_params=_params(),
        name="mlstm_core",
    )(qvo, kt, qvo, qvo, grow, norm_w.reshape(1, dvt), triu3, x, vec, w_out16)


def _mb_in_kernel(x_ref, vec_ref, wzx_ref, wd_ref, z_ref, xbc_ref, dtc_ref, dtr_ref):
    di = z_ref.shape[1]
    for rows in _halves(x_ref):
        h = _prenorm_mod(x_ref[rows, :], vec_ref[...]).astype(BF16)
        for j in range(di // 512):
            cols = slice(j * 512, (j + 1) * 512)
            z_ref[rows, cols] = _silu(_dot(h, wzx_ref[:, cols])).astype(BF16)
        for j in range(xbc_ref.shape[1] // 512):
            cols = slice(j * 512, (j + 1) * 512)
            xbc_ref[rows, cols] = _dot(h, wzx_ref[:, di + j * 512:di + (j + 1) * 512]).astype(BF16)
        dt = _dot(h, wd_ref[...])
        dtc_ref[rows, :] = dt
        dtr_ref[:, rows] = dt.T[0:dtr_ref.shape[0], :]


def _split_bf16(v, terms):
    parts = []
    for _ in range(terms):
        part = v.astype(BF16)
        parts.append(part)
        v = v - part.astype(F32)
    return parts


def _mb_core_kernel(z_ref, xbc_ref, dtc_ref, dtr_ref, cw_ref, cb_ref, dbc_ref, dbr_ref, ac_ref, ar_ref,
                    e_ref, dsk_ref, nw_ref, shift_ref, tril_ref, triu_ref, x_ref, vec_ref, wout_ref,
                    out_ref, xext_ref, xa_ref, st_ref, ys_ref, *, di, groups, nstate, hpg, p):
    Q = CHUNK

    @pl.when(pl.program_id(0) == 0)
    def _():
        xext_ref[Q:2 * Q, :] = jnp.zeros((Q, xext_ref.shape[1]), BF16)
        st_ref[...] = jnp.zeros_like(st_ref)

    for c in range(z_ref.shape[0] // Q):
        t = slice(c * Q, (c + 1) * Q)
        _mb_chunk(z_ref, xbc_ref, dtc_ref, dtr_ref, cw_ref, cb_ref, dbc_ref, dbr_ref, ac_ref, ar_ref,
                  e_ref, dsk_ref, nw_ref, shift_ref, tril_ref, triu_ref, xext_ref, xa_ref, st_ref, ys_ref,
                  t, di, groups, nstate, hpg, p)
        y = _dot(ys_ref[...], wout_ref[...])
        out_ref[t, :] = _postnorm_residual(x_ref[t, :], y, vec_ref[...], 1.0)


def _mb_chunk(z_ref, xbc_ref, dtc_ref, dtr_ref, cw_ref, cb_ref, dbc_ref, dbr_ref, ac_ref, ar_ref,
              e_ref, dsk_ref, nw_ref, shift_ref, tril_ref, triu_ref, xext_ref, xa_ref, st_ref, ys_ref,
              t, di, groups, nstate, hpg, p):
    Q = CHUNK
    gw = hpg * p
    conv_dim = di + 2 * groups * nstate

    xext_ref[0:Q, :] = xext_ref[Q:2 * Q, :]
    xext_ref[Q:2 * Q, :] = xbc_ref[t, :]

    def conv_silu(first_col, width):
        cols = slice(first_col, first_col + width)
        shifted = _dot(shift_ref[...], xext_ref[:, cols])
        acc = cb_ref[:, cols] + cw_ref[MB_CONV - 1:MB_CONV, cols] * xbc_ref[t, cols].astype(F32)
        for k in range(MB_CONV - 1):
            acc = acc + cw_ref[k:k + 1, cols] * shifted[k * Q:(k + 1) * Q]
        xa_ref[:, cols] = _silu(acc)

    for first_col in range(di, conv_dim, gw):
        conv_silu(first_col, gw)
    conv_silu(0, gw)

    row = lax.broadcasted_iota(jnp.int32, (Q, Q), 0)
    col = lax.broadcasted_iota(jnp.int32, (Q, Q), 1)
    causal = col <= row

    dt_c = _softplus(dtc_ref[t, :] + dbc_ref[...])
    dt_r = _softplus(dtr_ref[:, t] + dbr_ref[...])
    da_c = jnp.concatenate(_split_bf16(dt_c * -jnp.exp(ac_ref[...]), 3), axis=0)
    da_r = jnp.concatenate(_split_bf16(dt_r * -jnp.exp(ar_ref[...]), 3), axis=1)
    a_c = _dot(tril_ref[...], da_c)
    a_r = _dot(da_r, triu_ref[...])
    a_last = a_c[Q - 1:Q, :]
    per_head = jnp.concatenate([dt_c, jnp.exp(a_c), jnp.exp(a_last - a_c),
                                jnp.broadcast_to(jnp.exp(a_last), (16, LANES))], axis=0)
    per_head = jnp.concatenate(_split_bf16(per_head, 2), axis=1)
    lane = lax.broadcasted_iota(jnp.int32, (Q, LANES), 1)
    a2_c, a2_r = a_c * LOG2_E, a_r * LOG2_E

    for g in range(groups):
        gcols = slice(g * gw, (g + 1) * gw)
        if g + 1 < groups:
            conv_silu((g + 1) * gw, gw)
        per_chan = _dot(per_head, e_ref[:, gcols])
        dt_x, ea_x, dec_x = per_chan[0:Q], per_chan[Q:2 * Q], per_chan[2 * Q:3 * Q]
        eal = per_chan[3 * Q:3 * Q + 1]
        xs = xa_ref[:, gcols]
        bm = xa_ref[:, di + g * nstate:di + (g + 1) * nstate]
        cm = xa_ref[:, di + groups * nstate + g * nstate:di + groups * nstate + (g + 1) * nstate]
        cm16 = cm.astype(BF16)
        xdt = xs * dt_x
        xdt16 = xdt.astype(BF16)
        cb = jnp.where(causal, _dot_nt(cm16, bm.astype(BF16)), 0.0)

        pairs = []
        for hp in range(hpg // 2):
            x_pair = xdt16[:, hp * LANES:(hp + 1) * LANES]
            ys = []
            for e in range(2):
                hd = g * hpg + hp * 2 + e
                seg = a2_c[:, hd:hd + 1] - a2_r[hd:hd + 1, :]
                lmat = (cb * jnp.exp2(jnp.minimum(seg, 0.0))).astype(BF16)
                ys.append(_dot(lmat, x_pair))
            pairs.append(jnp.where(lane < p, ys[0], ys[1]))
        y = jnp.concatenate(pairs, axis=1)

        st = st_ref[g]
        y = y + _dot(cm16, st.astype(BF16)) * ea_x
        y = y + dsk_ref[:, gcols] * xs

        xdec = (xdt * dec_x).astype(BF16)
        st_ref[g] = eal * st + _dot(bm.T.astype(BF16), xdec)

        y = y * z_ref[t, gcols].astype(F32)
        y = y * lax.rsqrt(jnp.mean(y * y, axis=-1, keepdims=True) + EPS)
        ys_ref[:, gcols] = (y * nw_ref[:, gcols]).astype(BF16)


def _mamba_sublayer(x, vec, w_in16, layer, conv_w, conv_b, dt_bias, a_log, d_skip, norm_w, w_out16):
    s, d = x.shape
    heads = dt_bias.shape[0]
    p, groups, nstate = MB_HEADDIM, MB_GROUPS, MB_STATE
    hpg = heads // groups
    di = heads * p
    conv_dim = di + 2 * groups * nstate
    tm = min(ROW_TILE, s)

    wd = jnp.pad(w_in16[layer, :, di + conv_dim:], ((0, 0), (0, LANES - heads)))

    z, xbc, dtc, dtr = pl.pallas_call(
        _mb_in_kernel,
        grid=(s // tm,),
        in_specs=[_row_spec(tm, d), _const_spec((8, d)), _layer_spec(layer, d, di + conv_dim, 0),
                  _const_spec((d, LANES))],
        out_specs=[_row_spec(tm, di), _row_spec(tm, conv_dim), _row_spec(tm, LANES), _col_spec(heads, tm)],
        out_shape=[jax.ShapeDtypeStruct((s, di), BF16), jax.ShapeDtypeStruct((s, conv_dim), BF16),
                   jax.ShapeDtypeStruct((s, LANES), F32), jax.ShapeDtypeStruct((heads, s), F32)],
        compiler_params=_params(),
        name="mamba_in",
    )(x, vec, w_in16, wd)

    pad = (0, LANES - heads)
    Q = CHUNK
    head_of_chan = jnp.arange(di)[None, :] // p
    expand = jnp.tile(jnp.arange(LANES)[:, None] == head_of_chan, (2, 1)).astype(BF16)
    t_out = jnp.arange((MB_CONV - 1) * Q)
    src = Q + t_out % Q - (MB_CONV - 1) + t_out // Q
    shift = (jnp.arange(2 * Q)[None, :] == src[:, None]).astype(BF16)
    tri = jnp.arange(Q)[None, :] <= jnp.arange(Q)[:, None]
    tril3 = jnp.tile(tri, (1, 3)).astype(BF16)
    triu3 = jnp.tile(tri.T, (3, 1)).astype(BF16)
    blk = min(MB_CHUNKS_PER_STEP * Q, s)
    return pl.pallas_call(
        functools.partial(_mb_core_kernel, di=di, groups=groups, nstate=nstate, hpg=hpg, p=p),
        grid=(s // blk,),
        in_specs=[_row_spec(blk, di), _row_spec(blk, conv_dim), _row_spec(blk, LANES), _col_spec(heads, blk),
                  _const_spec((MB_CONV, conv_dim)), _const_spec((1, conv_dim)),
                  _const_spec((1, LANES)), _const_spec((heads, 1)),
                  _const_spec((1, LANES)), _const_spec((heads, 1)),
                  _const_spec((2 * LANES, di)), _const_spec((1, di)), _const_spec((1, di)),
                  _const_spec(((MB_CONV - 1) * Q, 2 * Q)), _const_spec((Q, 3 * Q)), _const_spec((3 * Q, Q)),
                  _row_spec(blk, d), _const_spec((8, d)), _layer_spec(layer, di, d, 0)],
        out_specs=_row_spec(blk, d),
        out_shape=jax.ShapeDtypeStruct((s, d), F32),
        scratch_shapes=[pltpu.VMEM((2 * Q, conv_dim), BF16), pltpu.VMEM((Q, conv_dim), F32),
                        pltpu.VMEM((groups, nstate, hpg * p), F32), pltpu.VMEM((Q, di), BF16)],
        compiler_params=_params(),
        name="mamba_core",
    )(z, xbc, dtc, dtr, conv_w, conv_b.reshape(1, conv_dim),
      jnp.pad(dt_bias, pad).reshape(1, LANES), dt_bias.reshape(heads, 1),
      jnp.pad(a_log, pad).reshape(1, LANES), a_log.reshape(heads, 1),
      expand, jnp.repeat(d_skip, p).reshape(1, di), norm_w.reshape(1, di), shift, tril3, triu3,
      x, vec, w_out16)


def kernel(x, c, ada_w, ada_b, norm_pre, norm_post, ffn_w1, ffn_w3, ffn_w2, ml_w_in, ml_b_gate, ml_norm_w,
           ml_w_out, mb_w_in, mb_conv_w, mb_conv_b, mb_dt_bias, mb_A_log, mb_D, mb_norm_w, mb_w_out):
    batch, seq, d = x.shape
    assert batch == 1, "the kernels assume one sequence"
    depth = ada_w.shape[0]
    mod = _ada_mod(c, ada_w, ada_b).reshape(depth, N_SUB, 3, d)
    pad_rows = jnp.zeros((3, d), F32)
    d_ff = ffn_w1.shape[-1]
    w1 = ffn_w1.astype(BF16).reshape(depth * 2, d, d_ff)
    w3 = ffn_w3.astype(BF16).reshape(depth * 2, d, d_ff)
    w2 = ffn_w2.astype(BF16).reshape(depth * 2, d_ff, d)
    ml_in16, ml_out16 = ml_w_in.astype(BF16), ml_w_out.astype(BF16)
    mb_in16, mb_out16 = mb_w_in.astype(BF16), mb_w_out.astype(BF16)
    h = x.reshape(seq, d)
    for i in range(depth):
        j = i // 2
        for sub in range(N_SUB):
            vec = jnp.concatenate([norm_pre[i, sub][None], mod[i, sub, 0][None], mod[i, sub, 1][None],
                                   norm_post[i, sub][None], mod[i, sub, 2][None], pad_rows], axis=0)
            if sub != 1:
                h = _ffn_sublayer(h, vec, w1, w3, w2, 2 * i + sub // 2, FFN_HALF)
            elif i % 2 == 0:
                h = _mlstm_sublayer(h, vec, ml_in16, j, ml_b_gate[j], ml_norm_w[j], ml_out16)
            else:
                h = _mamba_sublayer(h, vec, mb_in16, j, mb_conv_w[j], mb_conv_b[j], mb_dt_bias[j],
                                    mb_A_log[j], mb_D[j], mb_norm_w[j], mb_out16)
    return h.reshape(batch, seq, d)
```

```python
import functools

import jax
import jax.numpy as jnp
from jax import lax
from jax.experimental import pallas as pl
from jax.experimental.pallas import tpu as pltpu

F32 = jnp.float32
BF16 = jnp.bfloat16
HIGHEST = lax.Precision.HIGHEST
EPS = 1e-6
LOG2_E = 1.4426950408889634

N_SUB = 3
FFN_HALF = 0.5
ML_HEADS = 4
MB_HEADDIM = 64
MB_GROUPS = 4
MB_STATE = 128
MB_CONV = 4

LANES = 128
MXU_N = 256
ROW_TILE = 512
CHUNK = 128
ML_CHUNKS_PER_STEP = 2
MB_CHUNKS_PER_STEP = 4
VMEM_LIMIT = 56 * 1024 * 1024

_NT = (((1,), (1,)), ((), ()))


def _dot(a, b, precision=None):
    return jnp.dot(a, b, preferred_element_type=F32, precision=precision)


def _dot_nt(a, b):
    return lax.dot_general(a, b, _NT, preferred_element_type=F32)


def _sigmoid(v):
    return 0.5 + 0.5 * jnp.tanh(0.5 * v)


def _silu(v):
    half = 0.5 * v
    return half + half * jnp.tanh(half)


def _softplus(v):
    return jnp.maximum(v, 0.0) + jnp.log(1.0 + jnp.exp(-jnp.abs(v)))


def _log_sigmoid(v):
    return -_softplus(-v)


def _prenorm_mod(x, vec):
    xn = x * lax.rsqrt(jnp.mean(x * x, axis=-1, keepdims=True) + EPS)
    return (xn * vec[0:1]) * (1.0 + vec[2:3]) + vec[1:2]


def _postnorm_residual(x, y, vec, weight):
    yn = y * lax.rsqrt(jnp.mean(y * y, axis=-1, keepdims=True) + EPS) * vec[3:4]
    return x + (weight * vec[4:5]) * yn


def _halves(ref):
    half = ref.shape[0] // 2
    return (slice(0, half), slice(half, 2 * half))


def _const_spec(shape):
    zeros = (0,) * len(shape)
    return pl.BlockSpec(shape, lambda i: zeros, pipeline_mode=pl.Buffered(1))


def _layer_spec(layer, rows, cols, col_block):
    return pl.BlockSpec((None, rows, cols), lambda i: (layer, 0, col_block), pipeline_mode=pl.Buffered(1))


def _row_spec(rows, cols):
    return pl.BlockSpec((rows, cols), lambda i: (i, 0))


def _col_spec(rows, cols):
    return pl.BlockSpec((rows, cols), lambda i: (0, i))


def _params():
    return pltpu.CompilerParams(dimension_semantics=("arbitrary",), vmem_limit_bytes=VMEM_LIMIT)


def _ada_kernel(c_ref, w_ref, b_ref, o_ref):
    o_ref[0] = jnp.sum(_silu(c_ref[...]) * w_ref[0], axis=0, keepdims=True) + b_ref[0]


def _ada_mod(c, ada_w, ada_b):
    depth, d, n = ada_w.shape
    tn = n // 6
    out = pl.pallas_call(
        _ada_kernel,
        grid=(depth, n // tn),
        in_specs=[pl.BlockSpec((d, 1), lambda i, j: (0, 0)),
                  pl.BlockSpec((1, d, tn), lambda i, j: (i, 0, j)),
                  pl.BlockSpec((1, 1, tn), lambda i, j: (i, 0, j))],
        out_specs=pl.BlockSpec((1, 1, tn), lambda i, j: (i, 0, j)),
        out_shape=jax.ShapeDtypeStruct((depth, 1, n), F32),
        compiler_params=pltpu.CompilerParams(dimension_semantics=("arbitrary", "arbitrary"),
                                             vmem_limit_bytes=VMEM_LIMIT),
        name="ada_mod",
    )(c.reshape(d, 1), ada_w, ada_b.reshape(depth, 1, n))
    return out[:, 0, :]


def _ffn_kernel(x_ref, vec_ref, w1_ref, w3_ref, w2_ref, o_ref, h_ref, act_ref, *, weight, d_ff):
    vec = vec_ref[...]
    for rows in _halves(x_ref):
        h_ref[rows, :] = _prenorm_mod(x_ref[rows, :], vec).astype(BF16)
    for j in range(d_ff // MXU_N):
        cols = slice(j * MXU_N, (j + 1) * MXU_N)
        w1 = w1_ref[:, cols].astype(BF16)
        w3 = w3_ref[:, cols].astype(BF16)
        for rows in _halves(x_ref):
            h = h_ref[rows, :]
            act_ref[rows, cols] = (_silu(_dot(h, w1)) * _dot(h, w3)).astype(BF16)
    for rows in _halves(x_ref):
        y = _dot(act_ref[rows, :], w2_ref[...])
        o_ref[rows, :] = _postnorm_residual(x_ref[rows, :], y, vec, weight)


def _ffn_sublayer(x, vec, w1, w3, w2, layer, weight):
    s, d = x.shape
    d_ff = w1.shape[2]
    tm = min(ROW_TILE, s)
    return pl.pallas_call(
        functools.partial(_ffn_kernel, weight=weight, d_ff=d_ff),
        grid=(s // tm,),
        in_specs=[_row_spec(tm, d), _const_spec((8, d)), _layer_spec(layer, d, d_ff, 0),
                  _layer_spec(layer, d, d_ff, 0), _layer_spec(layer, d_ff, d, 0)],
        out_specs=_row_spec(tm, d),
        out_shape=jax.ShapeDtypeStruct((s, d), F32),
        scratch_shapes=[pltpu.VMEM((tm, d), BF16), pltpu.VMEM((tm, d_ff), BF16)],
        compiler_params=_params(),
        name="ffn",
    )(x, vec, w1, w3, w2)


def _ml_in_kernel(x_ref, vec_ref, wq_ref, wk_ref, wv_ref, wo_ref, wg_ref, bg_ref,
                  voq_ref, kt_ref, grow_ref, *, k_scale):
    for rows in _halves(x_ref):
        h = _prenorm_mod(x_ref[rows, :], vec_ref[...]).astype(BF16)
        base = 0
        for w_ref, act in ((wv_ref, None), (wo_ref, _sigmoid), (wq_ref, None)):
            for j in range(w_ref.shape[1] // 512):
                r = _dot(h, w_ref[:, j * 512:(j + 1) * 512])
                voq_ref[rows, base:base + 512] = (r if act is None else act(r)).astype(BF16)
                base += 512
        kt_ref[:, rows] = (_dot(h, wk_ref[...]) * k_scale).T.astype(BF16)
        grow_ref[:, rows] = (_dot(h, wg_ref[...]) + bg_ref[...]).T[0:grow_ref.shape[0], :]


def _lane_tile(a, n):
    return jnp.concatenate([a] * n, axis=1)


def _ml_core_kernel(voq_ref, kt_ref, grow_ref, nw_ref, triu_ref, x_ref, vec_ref, wout_ref,
                    out_ref, c_ref, m_ref, hs_ref, *, heads, dk, dv):
    L = CHUNK
    assert L == LANES

    @pl.when(pl.program_id(0) == 0)
    def _():
        c_ref[...] = jnp.zeros_like(c_ref)
        m_ref[...] = jnp.zeros_like(m_ref)

    row = lax.broadcasted_iota(jnp.int32, (L, L), 0)
    col = lax.broadcasted_iota(jnp.int32, (L, L), 1)
    causal = col <= row

    ones_blk = jnp.ones((L, LANES), BF16)
    sub = lax.broadcasted_iota(jnp.int32, (grow_ref.shape[0], L), 0)
    is_fg = (sub >= heads) & (sub < 2 * heads)

    dvt = heads * dv
    for c in range(voq_ref.shape[0] // L):
        t = slice(c * L, (c + 1) * L)
        gr = grow_ref[:, t]
        lf_all = jnp.where(is_fg, _log_sigmoid(gr), 0.0)
        b_all = _dot(jnp.concatenate(_split_bf16(lf_all, 3), axis=1), triu_ref[...])

        for h in range(heads):
            hv = slice(h * dv, (h + 1) * dv)
            q = voq_ref[t, 2 * dvt + h * dk:2 * dvt + (h + 1) * dk]
            kt = kt_ref[h * dk:(h + 1) * dk, t]
            v_aug = jnp.concatenate([voq_ref[t, hv], ones_blk], axis=1)
            lf_r = lf_all[heads + h:heads + h + 1, :]
            r = gr[h:h + 1, :] - b_all[heads + h:heads + h + 1, :]
            m_prev = m_ref[h:h + 1, :]

            b_c = jnp.sum(jnp.where(causal, lf_r, 0.0), axis=1, keepdims=True)
            r_mask = jnp.where(causal, r, -jnp.inf)
            g = jnp.maximum(jnp.max(r_mask, axis=1, keepdims=True), m_prev)
            s = _dot(q, kt) * jnp.exp(r_mask - g)
            inter = jnp.exp(m_prev - g)
            c_old = c_ref[h]
            numden = (_dot(s.astype(BF16), v_aug)
                      + _lane_tile(inter, dv // LANES + 1) * _dot(q, c_old.astype(BF16)))
            den = jnp.maximum(jnp.abs(numden[:, dv:]), jnp.exp(-(b_c + g)))
            hh = numden[:, :dv] * _lane_tile(1.0 / den, dv // LANES)

            g_last = g[L - 1:L, :]
            w_r = jnp.exp(r - g_last)
            decay = jnp.exp(m_prev - g_last)
            ktw = (kt.astype(F32) * w_r).astype(BF16)
            c_ref[h] = _lane_tile(decay, dv // LANES + 1) * c_old + _dot(ktw, v_aug)
            m_ref[h:h + 1, :] = jnp.sum(lf_r, axis=1, keepdims=True) + g_last

            hn = hh * lax.rsqrt(jnp.mean(hh * hh, axis=-1, keepdims=True) + EPS)
            gate = voq_ref[t, dvt + h * dv:dvt + (h + 1) * dv].astype(F32)
            hs_ref[t, hv] = (hn * nw_ref[:, hv] * gate).astype(BF16)

        y = _dot(hs_ref[t, :], wout_ref[...])
        out_ref[t, :] = _postnorm_residual(x_ref[t, :], y, vec_ref[...], 1.0)


def _mlstm_sublayer(x, vec, w_in16, layer, b_gate, norm_w, w_out16):
    s, d = x.shape
    heads = ML_HEADS
    qk = (w_in16.shape[2] - 2 * heads) // 6
    dvt = 2 * qk
    dk, dv = qk // heads, dvt // heads
    tm = min(ROW_TILE, s)

    wg = jnp.pad(w_in16[layer, :, 2 * qk + 2 * dvt:], ((0, 0), (0, LANES - 2 * heads)))
    bg = jnp.pad(b_gate, (0, LANES - 2 * heads)).reshape(1, LANES)
    n_main = qk + 2 * dvt

    voq, kt, grow = pl.pallas_call(
        functools.partial(_ml_in_kernel, k_scale=dk ** -0.5),
        grid=(s // tm,),
        in_specs=[_row_spec(tm, d), _const_spec((8, d)),
                  _layer_spec(layer, d, qk, 0), _layer_spec(layer, d, qk, 1),
                  _layer_spec(layer, d, dvt, 1), _layer_spec(layer, d, dvt, 2),
                  _const_spec((d, LANES)), _const_spec((1, LANES))],
        out_specs=[_row_spec(tm, n_main), _col_spec(qk, tm), _col_spec(16, tm)],
        out_shape=[jax.ShapeDtypeStruct((s, n_main), BF16), jax.ShapeDtypeStruct((qk, s), BF16),
                   jax.ShapeDtypeStruct((16, s), F32)],
        compiler_params=_params(),
        name="mlstm_in",
    )(x, vec, w_in16, w_in16, w_in16, w_in16, wg, bg)

    L = CHUNK
    tri = jnp.arange(L)[None, :] >= jnp.arange(L)[:, None]
    triu3 = jnp.tile(tri, (3, 1)).astype(BF16)
    blk = min(ML_CHUNKS_PER_STEP * L, s)
    return pl.pallas_call(
        functools.partial(_ml_core_kernel, heads=heads, dk=dk, dv=dv),
        grid=(s // blk,),
        in_specs=[_row_spec(blk, n_main), _col_spec(qk, blk),
                  _col_spec(16, blk), _const_spec((1, dvt)), _const_spec((3 * L, L)),
                  _row_spec(blk, d), _const_spec((8, d)), _layer_spec(layer, dvt, d, 0)],
        out_specs=_row_spec(blk, d),
        out_shape=jax.ShapeDtypeStruct((s, d), F32),
        scratch_shapes=[pltpu.VMEM((heads, dk, dv + LANES), F32), pltpu.VMEM((8, LANES), F32),
                        pltpu.VMEM((blk, dvt), BF16)],
        compiler_params=_params(),
        name="mlstm_core",
    )(voq, kt, grow, norm_w.reshape(1, dvt), triu3, x, vec, w_out16)


def _mb_in_kernel(x_ref, vec_ref, wzx_ref, wd_ref, z_ref, xbc_ref, dtc_ref, dtr_ref):
    di = z_ref.shape[1]
    for rows in _halves(x_ref):
        h = _prenorm_mod(x_ref[rows, :], vec_ref[...]).astype(BF16)
        for j in range(di // 512):
            cols = slice(j * 512, (j + 1) * 512)
            z_ref[rows, cols] = _silu(_dot(h, wzx_ref[:, cols])).astype(BF16)
        for j in range(xbc_ref.shape[1] // 512):
            cols = slice(j * 512, (j + 1) * 512)
            xbc_ref[rows, cols] = _dot(h, wzx_ref[:, di + j * 512:di + (j + 1) * 512]).astype(BF16)
        dt = _dot(h, wd_ref[...])
        dtc_ref[rows, :] = dt
        dtr_ref[:, rows] = dt.T[0:dtr_ref.shape[0], :]


def _split_bf16(v, terms):
    parts = []
    for _ in range(terms):
        part = v.astype(BF16)
        parts.append(part)
        v = v - part.astype(F32)
    return parts


def _mb_core_kernel(z_ref, xbc_ref, dtc_ref, dtr_ref, cw_ref, cb_ref, dbc_ref, dbr_ref, ac_ref, ar_ref,
                    e_ref, dsk_ref, nw_ref, shift_ref, tril_ref, triu_ref, x_ref, vec_ref, wout_ref,
                    out_ref, xext_ref, xa_ref, st_ref, ys_ref, *, di, groups, nstate, hpg, p):
    Q = CHUNK

    @pl.when(pl.program_id(0) == 0)
    def _():
        xext_ref[Q:2 * Q, :] = jnp.zeros((Q, xext_ref.shape[1]), BF16)
        st_ref[...] = jnp.zeros_like(st_ref)

    for c in range(z_ref.shape[0] // Q):
        t = slice(c * Q, (c + 1) * Q)
        _mb_chunk(z_ref, xbc_ref, dtc_ref, dtr_ref, cw_ref, cb_ref, dbc_ref, dbr_ref, ac_ref, ar_ref,
                  e_ref, dsk_ref, nw_ref, shift_ref, tril_ref, triu_ref, xext_ref, xa_ref, st_ref, ys_ref,
                  t, di, groups, nstate, hpg, p)
        y = _dot(ys_ref[...], wout_ref[...])
        out_ref[t, :] = _postnorm_residual(x_ref[t, :], y, vec_ref[...], 1.0)


def _mb_chunk(z_ref, xbc_ref, dtc_ref, dtr_ref, cw_ref, cb_ref, dbc_ref, dbr_ref, ac_ref, ar_ref,
              e_ref, dsk_ref, nw_ref, shift_ref, tril_ref, triu_ref, xext_ref, xa_ref, st_ref, ys_ref,
              t, di, groups, nstate, hpg, p):
    Q = CHUNK
    gw = hpg * p
    conv_dim = di + 2 * groups * nstate

    xext_ref[0:Q, :] = xext_ref[Q:2 * Q, :]
    xext_ref[Q:2 * Q, :] = xbc_ref[t, :]

    def conv_silu(first_col, width):
        cols = slice(first_col, first_col + width)
        shifted = _dot(shift_ref[...], xext_ref[:, cols])
        acc = cb_ref[:, cols] + cw_ref[MB_CONV - 1:MB_CONV, cols] * xbc_ref[t, cols].astype(F32)
        for k in range(MB_CONV - 1):
            acc = acc + cw_ref[k:k + 1, cols] * shifted[k * Q:(k + 1) * Q]
        xa_ref[:, cols] = _silu(acc)

    for first_col in range(di, conv_dim, gw):
        conv_silu(first_col, gw)
    conv_silu(0, gw)

    row = lax.broadcasted_iota(jnp.int32, (Q, Q), 0)
    col = lax.broadcasted_iota(jnp.int32, (Q, Q), 1)
    causal = col <= row

    dt_c = _softplus(dtc_ref[t, :] + dbc_ref[...])
    dt_r = _softplus(dtr_ref[:, t] + dbr_ref[...])
    da_c = jnp.concatenate(_split_bf16(dt_c * -jnp.exp(ac_ref[...]), 3), axis=0)
    da_r = jnp.concatenate(_split_bf16(dt_r * -jnp.exp(ar_ref[...]), 3), axis=1)
    a_c = _dot(tril_ref[...], da_c)
    a_r = _dot(da_r, triu_ref[...])
    a_last = a_c[Q - 1:Q, :]
    per_head = jnp.concatenate([dt_c, jnp.exp(a_c), jnp.exp(a_last - a_c),
                                jnp.broadcast_to(jnp.exp(a_last), (16, LANES))], axis=0)
    per_head = jnp.concatenate(_split_bf16(per_head, 2), axis=1)
    lane = lax.broadcasted_iota(jnp.int32, (Q, LANES), 1)
    a2_c, a2_r = a_c * LOG2_E, a_r * LOG2_E

    for g in range(groups):
        gcols = slice(g * gw, (g + 1) * gw)
        if g + 1 < groups:
            conv_silu((g + 1) * gw, gw)
        per_chan = _dot(per_head, e_ref[:, gcols])
        dt_x, ea_x, dec_x = per_chan[0:Q], per_chan[Q:2 * Q], per_chan[2 * Q:3 * Q]
        eal = per_chan[3 * Q:3 * Q + 1]
        xs = xa_ref[:, gcols]
        bm = xa_ref[:, di + g * nstate:di + (g + 1) * nstate]
        cm = xa_ref[:, di + groups * nstate + g * nstate:di + groups * nstate + (g + 1) * nstate]
        cm16 = cm.astype(BF16)
        xdt = xs * dt_x
        xdt16 = xdt.astype(BF16)
        cb = jnp.where(causal, _dot_nt(cm16, bm.astype(BF16)), 0.0)

        pairs = []
        for hp in range(hpg // 2):
            x_pair = xdt16[:, hp * LANES:(hp + 1) * LANES]
            ys = []
            for e in range(2):
                hd = g * hpg + hp * 2 + e
                seg = a2_c[:, hd:hd + 1] - a2_r[hd:hd + 1, :]
                lmat = (cb * jnp.exp2(jnp.minimum(seg, 0.0))).astype(BF16)
                ys.append(_dot(lmat, x_pair))
            pairs.append(jnp.where(lane < p, ys[0], ys[1]))
        y = jnp.concatenate(pairs, axis=1)

        st = st_ref[g]
        y = y + _dot(cm16, st.astype(BF16)) * ea_x
        y = y + dsk_ref[:, gcols] * xs

        xdec = (xdt * dec_x).astype(BF16)
        st_ref[g] = eal * st + _dot(bm.T.astype(BF16), xdec)

        y = y * z_ref[t, gcols].astype(F32)
        y = y * lax.rsqrt(jnp.mean(y * y, axis=-1, keepdims=True) + EPS)
        ys_ref[:, gcols] = (y * nw_ref[:, gcols]).astype(BF16)


def _mamba_sublayer(x, vec, w_in16, layer, conv_w, conv_b, dt_bias, a_log, d_skip, norm_w, w_out16):
    s, d = x.shape
    heads = dt_bias.shape[0]
    p, groups, nstate = MB_HEADDIM, MB_GROUPS, MB_STATE
    hpg = heads // groups
    di = heads * p
    conv_dim = di + 2 * groups * nstate
    tm = min(ROW_TILE, s)

    wd = jnp.pad(w_in16[layer, :, di + conv_dim:], ((0, 0), (0, LANES - heads)))

    z, xbc, dtc, dtr = pl.pallas_call(
        _mb_in_kernel,
        grid=(s // tm,),
        in_specs=[_row_spec(tm, d), _const_spec((8, d)), _layer_spec(layer, d, di + conv_dim, 0),
                  _const_spec((d, LANES))],
        out_specs=[_row_spec(tm, di), _row_spec(tm, conv_dim), _row_spec(tm, LANES), _col_spec(heads, tm)],
        out_shape=[jax.ShapeDtypeStruct((s, di), BF16), jax.ShapeDtypeStruct((s, conv_dim), BF16),
                   jax.ShapeDtypeStruct((s, LANES), F32), jax.ShapeDtypeStruct((heads, s), F32)],
        compiler_params=_params(),
        name="mamba_in",
    )(x, vec, w_in16, wd)

    pad = (0, LANES - heads)
    Q = CHUNK
    head_of_chan = jnp.arange(di)[None, :] // p
    expand = jnp.tile(jnp.arange(LANES)[:, None] == head_of_chan, (2, 1)).astype(BF16)
    t_out = jnp.arange((MB_CONV - 1) * Q)
    src = Q + t_out % Q - (MB_CONV - 1) + t_out // Q
    shift = (jnp.arange(2 * Q)[None, :] == src[:, None]).astype(BF16)
    tri = jnp.arange(Q)[None, :] <= jnp.arange(Q)[:, None]
    tril3 = jnp.tile(tri, (1, 3)).astype(BF16)
    triu3 = jnp.tile(tri.T, (3, 1)).astype(BF16)
    blk = min(MB_CHUNKS_PER_STEP * Q, s)
    return pl.pallas_call(
        functools.partial(_mb_core_kernel, di=di, groups=groups, nstate=nstate, hpg=hpg, p=p),
        grid=(s // blk,),
        in_specs=[_row_spec(blk, di), _row_spec(blk, conv_dim), _row_spec(blk, LANES), _col_spec(heads, blk),
                  _const_spec((MB_CONV, conv_dim)), _const_spec((1, conv_dim)),
                  _const_spec((1, LANES)), _const_spec((heads, 1)),
                  _const_spec((1, LANES)), _const_spec((heads, 1)),
                  _const_spec((2 * LANES, di)), _const_spec((1, di)), _const_spec((1, di)),
                  _const_spec(((MB_CONV - 1) * Q, 2 * Q)), _const_spec((Q, 3 * Q)), _const_spec((3 * Q, Q)),
                  _row_spec(blk, d), _const_spec((8, d)), _layer_spec(layer, di, d, 0)],
        out_specs=_row_spec(blk, d),
        out_shape=jax.ShapeDtypeStruct((s, d), F32),
        scratch_shapes=[pltpu.VMEM((2 * Q, conv_dim), BF16), pltpu.VMEM((Q, conv_dim), F32),
                        pltpu.VMEM((groups, nstate, hpg * p), F32), pltpu.VMEM((Q, di), BF16)],
        compiler_params=_params(),
        name="mamba_core",
    )(z, xbc, dtc, dtr, conv_w, conv_b.reshape(1, conv_dim),
      jnp.pad(dt_bias, pad).reshape(1, LANES), dt_bias.reshape(heads, 1),
      jnp.pad(a_log, pad).reshape(1, LANES), a_log.reshape(heads, 1),
      expand, jnp.repeat(d_skip, p).reshape(1, di), norm_w.reshape(1, di), shift, tril3, triu3,
      x, vec, w_out16)


def kernel(x, c, ada_w, ada_b, norm_pre, norm_post, ffn_w1, ffn_w3, ffn_w2, ml_w_in, ml_b_gate, ml_norm_w,
           ml_w_out, mb_w_in, mb_conv_w, mb_conv_b, mb_dt_bias, mb_A_log, mb_D, mb_norm_w, mb_w_out):
    batch, seq, d = x.shape
    assert batch == 1, "the kernels assume one sequence"
    depth = ada_w.shape[0]
    mod = _ada_mod(c, ada_w, ada_b).reshape(depth, N_SUB, 3, d)
    pad_rows = jnp.zeros((3, d), F32)
    d_ff = ffn_w1.shape[-1]
    w1 = ffn_w1.reshape(depth * 2, d, d_ff)
    w3 = ffn_w3.reshape(depth * 2, d, d_ff)
    w2 = ffn_w2.astype(BF16).reshape(depth * 2, d_ff, d)
    ml_in16, ml_out16 = ml_w_in.astype(BF16), ml_w_out.astype(BF16)
    mb_in16, mb_out16 = mb_w_in.astype(BF16), mb_w_out.astype(BF16)
    h = x.reshape(seq, d)
    for i in range(depth):
        j = i // 2
        for sub in range(N_SUB):
            vec = jnp.concatenate([norm_pre[i, sub][None], mod[i, sub, 0][None], mod[i, sub, 1][None],
                                   norm_post[i, sub][None], mod[i, sub, 2][None], pad_rows], axis=0)
            if sub != 1:
                h = _ffn_sublayer(h, vec, w1, w3, w2, 2 * i + sub // 2, FFN_HALF)
            elif i % 2 == 0:
                h = _mlstm_sublayer(h, vec, ml_in16, j, ml_b_gate[j], ml_norm_w[j], ml_out16)
            else:
                h = _mamba_sublayer(h, vec, mb_in16, j, mb_conv_w[j], mb_conv_b[j], mb_dt_bias[j],
                                    mb_A_log[j], mb_D[j], mb_norm_w[j], mb_out16)
    return h.reshape(batch, seq, d)
```

```python
import functools

import jax
import jax.numpy as jnp
from jax import lax
from jax.experimental import pallas as pl
from jax.experimental.pallas import tpu as pltpu

F32 = jnp.float32
BF16 = jnp.bfloat16
HIGHEST = lax.Precision.HIGHEST
EPS = 1e-6
LOG2_E = 1.4426950408889634

N_SUB = 3
FFN_HALF = 0.5
ML_HEADS = 4
MB_HEADDIM = 64
MB_GROUPS = 4
MB_STATE = 128
MB_CONV = 4

LANES = 128
MXU_N = 256
ROW_TILE = 512
IN_ROW_TILE = 1024
CHUNK = 128
ML_CHUNKS_PER_STEP = 2
MB_CHUNKS_PER_STEP = 4
VMEM_LIMIT = 56 * 1024 * 1024

_NT = (((1,), (1,)), ((), ()))


def _dot(a, b, precision=None):
    return jnp.dot(a, b, preferred_element_type=F32, precision=precision)


def _dot_nt(a, b):
    return lax.dot_general(a, b, _NT, preferred_element_type=F32)


def _sigmoid(v):
    return 0.5 + 0.5 * jnp.tanh(0.5 * v)


def _silu(v):
    half = 0.5 * v
    return half + half * jnp.tanh(half)


def _softplus(v):
    return jnp.maximum(v, 0.0) + jnp.log(1.0 + jnp.exp(-jnp.abs(v)))


def _log_sigmoid(v):
    return -_softplus(-v)


def _prenorm_mod(x, vec):
    xn = x * lax.rsqrt(jnp.mean(x * x, axis=-1, keepdims=True) + EPS)
    return (xn * vec[0:1]) * (1.0 + vec[2:3]) + vec[1:2]


def _postnorm_residual(x, y, vec, weight):
    yn = y * lax.rsqrt(jnp.mean(y * y, axis=-1, keepdims=True) + EPS) * vec[3:4]
    return x + (weight * vec[4:5]) * yn


def _halves(ref):
    half = ref.shape[0] // 2
    return (slice(0, half), slice(half, 2 * half))


def _store_time_major(ref, rows, value_t):
    piece = ref.shape[2]
    first, count = rows.start // piece, (rows.stop - rows.start) // piece
    for u in range(count):
        ref[first + u] = value_t[:, u * piece:(u + 1) * piece]


def _time_major_out(channels, tm, piece, dtype, s):
    assert (tm // 2) % piece == 0 and s % tm == 0
    return (pl.BlockSpec((tm // piece, channels, piece), lambda i: (i, 0, 0)),
            jax.ShapeDtypeStruct((s // piece, channels, piece), dtype))


def _time_major_in(channels, piece):
    return pl.BlockSpec((None, channels, piece), lambda i: (i, 0, 0))


def _const_spec(shape):
    zeros = (0,) * len(shape)
    return pl.BlockSpec(shape, lambda i: zeros, pipeline_mode=pl.Buffered(1))


def _layer_spec(layer, rows, cols, col_block):
    return pl.BlockSpec((None, rows, cols), lambda i: (layer, 0, col_block), pipeline_mode=pl.Buffered(1))


def _row_spec(rows, cols):
    return pl.BlockSpec((rows, cols), lambda i: (i, 0))


def _params():
    return pltpu.CompilerParams(dimension_semantics=("arbitrary",), vmem_limit_bytes=VMEM_LIMIT)


def _ada_kernel(c_ref, w_ref, b_ref, o_ref):
    o_ref[0] = jnp.sum(_silu(c_ref[...]) * w_ref[0], axis=0, keepdims=True) + b_ref[0]


def _ada_mod(c, ada_w, ada_b):
    depth, d, n = ada_w.shape
    tn = n // 6
    out = pl.pallas_call(
        _ada_kernel,
        grid=(depth, n // tn),
        in_specs=[pl.BlockSpec((d, 1), lambda i, j: (0, 0)),
                  pl.BlockSpec((1, d, tn), lambda i, j: (i, 0, j)),
                  pl.BlockSpec((1, 1, tn), lambda i, j: (i, 0, j))],
        out_specs=pl.BlockSpec((1, 1, tn), lambda i, j: (i, 0, j)),
        out_shape=jax.ShapeDtypeStruct((depth, 1, n), F32),
        compiler_params=pltpu.CompilerParams(dimension_semantics=("arbitrary", "arbitrary"),
                                             vmem_limit_bytes=VMEM_LIMIT),
        name="ada_mod",
    )(c.reshape(d, 1), ada_w, ada_b.reshape(depth, 1, n))
    return out[:, 0, :]


def _ffn_kernel(x_ref, vec_ref, w1_ref, w3_ref, w2_ref, o_ref, h_ref, act_ref, *, weight, d_ff):
    vec = vec_ref[...]
    for rows in _halves(x_ref):
        h_ref[rows, :] = _prenorm_mod(x_ref[rows, :], vec).astype(BF16)
    for j in range(d_ff // MXU_N):
        cols = slice(j * MXU_N, (j + 1) * MXU_N)
        w1 = w1_ref[:, cols].astype(BF16)
        w3 = w3_ref[:, cols].astype(BF16)
        for rows in _halves(x_ref):
            h = h_ref[rows, :]
            act_ref[rows, cols] = (_silu(_dot(h, w1)) * _dot(h, w3)).astype(BF16)
    for rows in _halves(x_ref):
        y = _dot(act_ref[rows, :], w2_ref[...])
        o_ref[rows, :] = _postnorm_residual(x_ref[rows, :], y, vec, weight)


def _ffn_sublayer(x, vec, w1, w3, w2, layer, weight):
    s, d = x.shape
    d_ff = w1.shape[2]
    tm = min(ROW_TILE, s)
    return pl.pallas_call(
        functools.partial(_ffn_kernel, weight=weight, d_ff=d_ff),
        grid=(s // tm,),
        in_specs=[_row_spec(tm, d), _const_spec((8, d)), _layer_spec(layer, d, d_ff, 0),
                  _layer_spec(layer, d, d_ff, 0), _layer_spec(layer, d_ff, d, 0)],
        out_specs=_row_spec(tm, d),
        out_shape=jax.ShapeDtypeStruct((s, d), F32),
        scratch_shapes=[pltpu.VMEM((tm, d), BF16), pltpu.VMEM((tm, d_ff), BF16)],
        compiler_params=_params(),
        name="ffn",
    )(x, vec, w1, w3, w2)


def _ml_in_kernel(x_ref, vec_ref, wq_ref, wk_ref, wv_ref, wo_ref, wg_ref, bg_ref,
                  voq_ref, kt_ref, grow_ref, *, k_scale):
    for rows in _halves(x_ref):
        h = _prenorm_mod(x_ref[rows, :], vec_ref[...]).astype(BF16)
        base = 0
        for w_ref, act in ((wv_ref, None), (wo_ref, _sigmoid), (wq_ref, None)):
            for j in range(w_ref.shape[1] // 512):
                r = _dot(h, w_ref[:, j * 512:(j + 1) * 512])
                voq_ref[rows, base:base + 512] = (r if act is None else act(r)).astype(BF16)
                base += 512
        _store_time_major(kt_ref, rows, (_dot(h, wk_ref[...]) * k_scale).T.astype(BF16))
        _store_time_major(grow_ref, rows, (_dot(h, wg_ref[...]) + bg_ref[...]).T[0:grow_ref.shape[1], :])


def _lane_tile(a, n):
    return jnp.concatenate([a] * n, axis=1)


def _ml_core_kernel(voq_ref, kt_ref, grow_ref, nw_ref, triu_ref, x_ref, vec_ref, wout_ref,
                    out_ref, c_ref, m_ref, hs_ref, *, heads, dk, dv):
    L = CHUNK
    assert L == LANES

    @pl.when(pl.program_id(0) == 0)
    def _():
        c_ref[...] = jnp.zeros_like(c_ref)
        m_ref[...] = jnp.zeros_like(m_ref)

    row = lax.broadcasted_iota(jnp.int32, (L, L), 0)
    col = lax.broadcasted_iota(jnp.int32, (L, L), 1)
    causal = col <= row

    ones_blk = jnp.ones((L, LANES), BF16)
    sub = lax.broadcasted_iota(jnp.int32, (grow_ref.shape[0], L), 0)
    is_fg = (sub >= heads) & (sub < 2 * heads)

    dvt = heads * dv
    for c in range(voq_ref.shape[0] // L):
        t = slice(c * L, (c + 1) * L)
        gr = grow_ref[:, t]
        lf_all = jnp.where(is_fg, _log_sigmoid(gr), 0.0)
        b_all = _dot(jnp.concatenate(_split_bf16(lf_all, 3), axis=1), triu_ref[...])

        for h in range(heads):
            hv = slice(h * dv, (h + 1) * dv)
            q = voq_ref[t, 2 * dvt + h * dk:2 * dvt + (h + 1) * dk]
            kt = kt_ref[h * dk:(h + 1) * dk, t]
            v_aug = jnp.concatenate([voq_ref[t, hv], ones_blk], axis=1)
            lf_r = lf_all[heads + h:heads + h + 1, :]
            r = gr[h:h + 1, :] - b_all[heads + h:heads + h + 1, :]
            m_prev = m_ref[h:h + 1, :]

            b_c = jnp.sum(jnp.where(causal, lf_r, 0.0), axis=1, keepdims=True)
            r_mask = jnp.where(causal, r, -jnp.inf)
            g = jnp.maximum(jnp.max(r_mask, axis=1, keepdims=True), m_prev)
            s = _dot(q, kt) * jnp.exp(r_mask - g)
            inter = jnp.exp(m_prev - g)
            c_old = c_ref[h]
            numden = (_dot(s.astype(BF16), v_aug)
                      + _lane_tile(inter, dv // LANES + 1) * _dot(q, c_old.astype(BF16)))
            den = jnp.maximum(jnp.abs(numden[:, dv:]), jnp.exp(-(b_c + g)))
            hh = numden[:, :dv] * _lane_tile(1.0 / den, dv // LANES)

            g_last = g[L - 1:L, :]
            w_r = jnp.exp(r - g_last)
            decay = jnp.exp(m_prev - g_last)
            ktw = (kt.astype(F32) * w_r).astype(BF16)
            c_ref[h] = _lane_tile(decay, dv // LANES + 1) * c_old + _dot(ktw, v_aug)
            m_ref[h:h + 1, :] = jnp.sum(lf_r, axis=1, keepdims=True) + g_last

            hn = hh * lax.rsqrt(jnp.mean(hh * hh, axis=-1, keepdims=True) + EPS)
            gate = voq_ref[t, dvt + h * dv:dvt + (h + 1) * dv].astype(F32)
            hs_ref[t, hv] = (hn * nw_ref[:, hv] * gate).astype(BF16)

        y = _dot(hs_ref[t, :], wout_ref[...])
        out_ref[t, :] = _postnorm_residual(x_ref[t, :], y, vec_ref[...], 1.0)


def _mlstm_sublayer(x, vec, w_in16, layer, b_gate, norm_w, w_out16):
    s, d = x.shape
    heads = ML_HEADS
    qk = (w_in16.shape[2] - 2 * heads) // 6
    dvt = 2 * qk
    dk, dv = qk // heads, dvt // heads
    tm = min(IN_ROW_TILE, s)

    wg = jnp.pad(w_in16[layer, :, 2 * qk + 2 * dvt:], ((0, 0), (0, LANES - 2 * heads)))
    bg = jnp.pad(b_gate, (0, LANES - 2 * heads)).reshape(1, LANES)
    n_main = qk + 2 * dvt
    L = CHUNK
    blk = min(ML_CHUNKS_PER_STEP * L, s)
    kt_spec, kt_shape = _time_major_out(qk, tm, blk, BF16, s)
    g_spec, g_shape = _time_major_out(16, tm, blk, F32, s)

    voq, kt, grow = pl.pallas_call(
        functools.partial(_ml_in_kernel, k_scale=dk ** -0.5),
        grid=(s // tm,),
        in_specs=[_row_spec(tm, d), _const_spec((8, d)),
                  _layer_spec(layer, d, qk, 0), _layer_spec(layer, d, qk, 1),
                  _layer_spec(layer, d, dvt, 1), _layer_spec(layer, d, dvt, 2),
                  _const_spec((d, LANES)), _const_spec((1, LANES))],
        out_specs=[_row_spec(tm, n_main), kt_spec, g_spec],
        out_shape=[jax.ShapeDtypeStruct((s, n_main), BF16), kt_shape, g_shape],
        compiler_params=_params(),
        name="mlstm_in",
    )(x, vec, w_in16, w_in16, w_in16, w_in16, wg, bg)

    tri = jnp.arange(L)[None, :] >= jnp.arange(L)[:, None]
    triu3 = jnp.tile(tri, (3, 1)).astype(BF16)
    return pl.pallas_call(
        functools.partial(_ml_core_kernel, heads=heads, dk=dk, dv=dv),
        grid=(s // blk,),
        in_specs=[_row_spec(blk, n_main), _time_major_in(qk, blk),
                  _time_major_in(16, blk), _const_spec((1, dvt)), _const_spec((3 * L, L)),
                  _row_spec(blk, d), _const_spec((8, d)), _layer_spec(layer, dvt, d, 0)],
        out_specs=_row_spec(blk, d),
        out_shape=jax.ShapeDtypeStruct((s, d), F32),
        scratch_shapes=[pltpu.VMEM((heads, dk, dv + LANES), F32), pltpu.VMEM((8, LANES), F32),
                        pltpu.VMEM((blk, dvt), BF16)],
        compiler_params=_params(),
        name="mlstm_core",
    )(voq, kt, grow, norm_w.reshape(1, dvt), triu3, x, vec, w_out16)


def _mb_in_kernel(x_ref, vec_ref, wzx_ref, wd_ref, z_ref, xbc_ref, dtc_ref, dtr_ref):
    di = z_ref.shape[1]
    for rows in _halves(x_ref):
        h = _prenorm_mod(x_ref[rows, :], vec_ref[...]).astype(BF16)
        for j in range(di // 512):
            cols = slice(j * 512, (j + 1) * 512)
            z_ref[rows, cols] = _silu(_dot(h, wzx_ref[:, cols])).astype(BF16)
        for j in range(xbc_ref.shape[1] // 512):
            cols = slice(j * 512, (j + 1) * 512)
            xbc_ref[rows, cols] = _dot(h, wzx_ref[:, di + j * 512:di + (j + 1) * 512]).astype(BF16)
        dt = _dot(h, wd_ref[...])
        dtc_ref[rows, :] = dt
        _store_time_major(dtr_ref, rows, dt.T[0:dtr_ref.shape[1], :])


def _split_bf16(v, terms):
    parts = []
    for _ in range(terms):
        part = v.astype(BF16)
        parts.append(part)
        v = v - part.astype(F32)
    return parts


def _mb_core_kernel(z_ref, xbc_ref, dtc_ref, dtr_ref, cw_ref, cb_ref, dbc_ref, dbr_ref, ac_ref, ar_ref,
                    e_ref, dsk_ref, nw_ref, shift_ref, tril_ref, triu_ref, x_ref, vec_ref, wout_ref,
                    out_ref, xext_ref, xa_ref, st_ref, ys_ref, *, di, groups, nstate, hpg, p):
    Q = CHUNK

    @pl.when(pl.program_id(0) == 0)
    def _():
        xext_ref[Q:2 * Q, :] = jnp.zeros((Q, xext_ref.shape[1]), BF16)
        st_ref[...] = jnp.zeros_like(st_ref)

    for c in range(z_ref.shape[0] // Q):
        t = slice(c * Q, (c + 1) * Q)
        _mb_chunk(z_ref, xbc_ref, dtc_ref, dtr_ref, cw_ref, cb_ref, dbc_ref, dbr_ref, ac_ref, ar_ref,
                  e_ref, dsk_ref, nw_ref, shift_ref, tril_ref, triu_ref, xext_ref, xa_ref, st_ref, ys_ref,
                  t, di, groups, nstate, hpg, p)
        y = _dot(ys_ref[...], wout_ref[...])
        out_ref[t, :] = _postnorm_residual(x_ref[t, :], y, vec_ref[...], 1.0)


def _mb_chunk(z_ref, xbc_ref, dtc_ref, dtr_ref, cw_ref, cb_ref, dbc_ref, dbr_ref, ac_ref, ar_ref,
              e_ref, dsk_ref, nw_ref, shift_ref, tril_ref, triu_ref, xext_ref, xa_ref, st_ref, ys_ref,
              t, di, groups, nstate, hpg, p):
    Q = CHUNK
    gw = hpg * p
    conv_dim = di + 2 * groups * nstate

    xext_ref[0:Q, :] = xext_ref[Q:2 * Q, :]
    xext_ref[Q:2 * Q, :] = xbc_ref[t, :]

    def conv_silu(first_col, width):
        cols = slice(first_col, first_col + width)
        shifted = _dot(shift_ref[...], xext_ref[:, cols])
        acc = cb_ref[:, cols] + cw_ref[MB_CONV - 1:MB_CONV, cols] * xbc_ref[t, cols].astype(F32)
        for k in range(MB_CONV - 1):
            acc = acc + cw_ref[k:k + 1, cols] * shifted[k * Q:(k + 1) * Q]
        xa_ref[:, cols] = _silu(acc)

    for first_col in range(di, conv_dim, gw):
        conv_silu(first_col, gw)
    conv_silu(0, gw)

    row = lax.broadcasted_iota(jnp.int32, (Q, Q), 0)
    col = lax.broadcasted_iota(jnp.int32, (Q, Q), 1)
    causal = col <= row

    dt_c = _softplus(dtc_ref[t, :] + dbc_ref[...])
    dt_r = _softplus(dtr_ref[:, t] + dbr_ref[...])
    da_c = jnp.concatenate(_split_bf16(dt_c * -jnp.exp(ac_ref[...]), 3), axis=0)
    da_r = jnp.concatenate(_split_bf16(dt_r * -jnp.exp(ar_ref[...]), 3), axis=1)
    a_c = _dot(tril_ref[...], da_c)
    a_r = _dot(da_r, triu_ref[...])
    a_last = a_c[Q - 1:Q, :]
    per_head = jnp.concatenate([dt_c, jnp.exp(a_c), jnp.exp(a_last - a_c),
                                jnp.broadcast_to(jnp.exp(a_last), (16, LANES))], axis=0)
    per_head = jnp.concatenate(_split_bf16(per_head, 2), axis=1)
    lane = lax.broadcasted_iota(jnp.int32, (Q, LANES), 1)
    a2_c, a2_r = a_c * LOG2_E, a_r * LOG2_E

    for g in range(groups):
        gcols = slice(g * gw, (g + 1) * gw)
        if g + 1 < groups:
            conv_silu((g + 1) * gw, gw)
        per_chan = _dot(per_head, e_ref[:, gcols])
        dt_x, ea_x, dec_x = per_chan[0:Q], per_chan[Q:2 * Q], per_chan[2 * Q:3 * Q]
        eal = per_chan[3 * Q:3 * Q + 1]
        xs = xa_ref[:, gcols]
        bm = xa_ref[:, di + g * nstate:di + (g + 1) * nstate]
        cm = xa_ref[:, di + groups * nstate + g * nstate:di + groups * nstate + (g + 1) * nstate]
        cm16 = cm.astype(BF16)
        xdt = xs * dt_x
        xdt16 = xdt.astype(BF16)
        cb = jnp.where(causal, _dot_nt(cm16, bm.astype(BF16)), 0.0)

        pairs = []
        for hp in range(hpg // 2):
            x_pair = xdt16[:, hp * LANES:(hp + 1) * LANES]
            ys = []
            for e in range(2):
                hd = g * hpg + hp * 2 + e
                seg = a2_c[:, hd:hd + 1] - a2_r[hd:hd + 1, :]
                lmat = (cb * jnp.exp2(jnp.minimum(seg, 0.0))).astype(BF16)
                ys.append(_dot(lmat, x_pair))
            pairs.append(jnp.where(lane < p, ys[0], ys[1]))
        y = jnp.concatenate(pairs, axis=1)

        st = st_ref[g]
        y = y + _dot(cm16, st.astype(BF16)) * ea_x
        y = y + dsk_ref[:, gcols] * xs

        xdec = (xdt * dec_x).astype(BF16)
        st_ref[g] = eal * st + _dot(bm.T.astype(BF16), xdec)

        y = y * z_ref[t, gcols].astype(F32)
        y = y * lax.rsqrt(jnp.mean(y * y, axis=-1, keepdims=True) + EPS)
        ys_ref[:, gcols] = (y * nw_ref[:, gcols]).astype(BF16)


def _mamba_sublayer(x, vec, w_in16, layer, conv_w, conv_b, dt_bias, a_log, d_skip, norm_w, w_out16):
    s, d = x.shape
    heads = dt_bias.shape[0]
    p, groups, nstate = MB_HEADDIM, MB_GROUPS, MB_STATE
    hpg = heads // groups
    di = heads * p
    conv_dim = di + 2 * groups * nstate
    tm = min(IN_ROW_TILE, s)

    wd = jnp.pad(w_in16[layer, :, di + conv_dim:], ((0, 0), (0, LANES - heads)))
    Q = CHUNK
    blk = min(MB_CHUNKS_PER_STEP * Q, s)
    dtr_spec, dtr_shape = _time_major_out(heads, tm, blk, F32, s)

    z, xbc, dtc, dtr = pl.pallas_call(
        _mb_in_kernel,
        grid=(s // tm,),
        in_specs=[_row_spec(tm, d), _const_spec((8, d)), _layer_spec(layer, d, di + conv_dim, 0),
                  _const_spec((d, LANES))],
        out_specs=[_row_spec(tm, di), _row_spec(tm, conv_dim), _row_spec(tm, LANES), dtr_spec],
        out_shape=[jax.ShapeDtypeStruct((s, di), BF16), jax.ShapeDtypeStruct((s, conv_dim), BF16),
                   jax.ShapeDtypeStruct((s, LANES), F32), dtr_shape],
        compiler_params=_params(),
        name="mamba_in",
    )(x, vec, w_in16, wd)

    pad = (0, LANES - heads)
    head_of_chan = jnp.arange(di)[None, :] // p
    expand = jnp.tile(jnp.arange(LANES)[:, None] == head_of_chan, (2, 1)).astype(BF16)
    t_out = jnp.arange((MB_CONV - 1) * Q)
    src = Q + t_out % Q - (MB_CONV - 1) + t_out // Q
    shift = (jnp.arange(2 * Q)[None, :] == src[:, None]).astype(BF16)
    tri = jnp.arange(Q)[None, :] <= jnp.arange(Q)[:, None]
    tril3 = jnp.tile(tri, (1, 3)).astype(BF16)
    triu3 = jnp.tile(tri.T, (3, 1)).astype(BF16)
    return pl.pallas_call(
        functools.partial(_mb_core_kernel, di=di, groups=groups, nstate=nstate, hpg=hpg, p=p),
        grid=(s // blk,),
        in_specs=[_row_spec(blk, di), _row_spec(blk, conv_dim), _row_spec(blk, LANES),
                  _time_major_in(heads, blk),
                  _const_spec((MB_CONV, conv_dim)), _const_spec((1, conv_dim)),
                  _const_spec((1, LANES)), _const_spec((heads, 1)),
                  _const_spec((1, LANES)), _const_spec((heads, 1)),
                  _const_spec((2 * LANES, di)), _const_spec((1, di)), _const_spec((1, di)),
                  _const_spec(((MB_CONV - 1) * Q, 2 * Q)), _const_spec((Q, 3 * Q)), _const_spec((3 * Q, Q)),
                  _row_spec(blk, d), _const_spec((8, d)), _layer_spec(layer, di, d, 0)],
        out_specs=_row_spec(blk, d),
        out_shape=jax.ShapeDtypeStruct((s, d), F32),
        scratch_shapes=[pltpu.VMEM((2 * Q, conv_dim), BF16), pltpu.VMEM((Q, conv_dim), F32),
                        pltpu.VMEM((groups, nstate, hpg * p), F32), pltpu.VMEM((Q, di), BF16)],
        compiler_params=_params(),
        name="mamba_core",
    )(z, xbc, dtc, dtr, conv_w, conv_b.reshape(1, conv_dim),
      jnp.pad(dt_bias, pad).reshape(1, LANES), dt_bias.reshape(heads, 1),
      jnp.pad(a_log, pad).reshape(1, LANES), a_log.reshape(heads, 1),
      expand, jnp.repeat(d_skip, p).reshape(1, di), norm_w.reshape(1, di), shift, tril3, triu3,
      x, vec, w_out16)


def kernel(x, c, ada_w, ada_b, norm_pre, norm_post, ffn_w1, ffn_w3, ffn_w2, ml_w_in, ml_b_gate, ml_norm_w,
           ml_w_out, mb_w_in, mb_conv_w, mb_conv_b, mb_dt_bias, mb_A_log, mb_D, mb_norm_w, mb_w_out):
    batch, seq, d = x.shape
    assert batch == 1, "the kernels assume one sequence"
    depth = ada_w.shape[0]
    mod = _ada_mod(c, ada_w, ada_b).reshape(depth, N_SUB, 3, d)
    pad_rows = jnp.zeros((3, d), F32)
    d_ff = ffn_w1.shape[-1]
    w1 = ffn_w1.reshape(depth * 2, d, d_ff)
    w3 = ffn_w3.reshape(depth * 2, d, d_ff)
    w2 = ffn_w2.astype(BF16).reshape(depth * 2, d_ff, d)
    ml_in16, ml_out16 = ml_w_in.astype(BF16), ml_w_out.astype(BF16)
    mb_in16, mb_out16 = mb_w_in.astype(BF16), mb_w_out.astype(BF16)
    h = x.reshape(seq, d)
    for i in range(depth):
        j = i // 2
        for sub in range(N_SUB):
            vec = jnp.concatenate([norm_pre[i, sub][None], mod[i, sub, 0][None], mod[i, sub, 1][None],
                                   norm_post[i, sub][None], mod[i, sub, 2][None], pad_rows], axis=0)
            if sub != 1:
                h = _ffn_sublayer(h, vec, w1, w3, w2, 2 * i + sub // 2, FFN_HALF)
            elif i % 2 == 0:
                h = _mlstm_sublayer(h, vec, ml_in16, j, ml_b_gate[j], ml_norm_w[j], ml_out16)
            else:
                h = _mamba_sublayer(h, vec, mb_in16, j, mb_conv_w[j], mb_conv_b[j], mb_dt_bias[j],
                                    mb_A_log[j], mb_D[j], mb_norm_w[j], mb_out16)
    return h.reshape(batch, seq, d)
```

```python
import functools

import jax
import jax.numpy as jnp
from jax import lax
from jax.experimental import pallas as pl
from jax.experimental.pallas import tpu as pltpu

F32 = jnp.float32
BF16 = jnp.bfloat16
HIGHEST = lax.Precision.HIGHEST
EPS = 1e-6
LOG2_E = 1.4426950408889634

N_SUB = 3
FFN_HALF = 0.5
ML_HEADS = 4
MB_HEADDIM = 64
MB_GROUPS = 4
MB_STATE = 128
MB_CONV = 4

LANES = 128
MXU_N = 256
ROW_TILE = 512
IN_ROW_TILE = 1024
CHUNK = 128
ML_CHUNKS_PER_STEP = 4
MB_CHUNKS_PER_STEP = 4
VMEM_LIMIT = 56 * 1024 * 1024

_NT = (((1,), (1,)), ((), ()))


def _dot(a, b, precision=None):
    return jnp.dot(a, b, preferred_element_type=F32, precision=precision)


def _dot_nt(a, b):
    return lax.dot_general(a, b, _NT, preferred_element_type=F32)


def _sigmoid(v):
    return 0.5 + 0.5 * jnp.tanh(0.5 * v)


def _silu(v):
    half = 0.5 * v
    return half + half * jnp.tanh(half)


def _softplus(v):
    return jnp.maximum(v, 0.0) + jnp.log(1.0 + jnp.exp(-jnp.abs(v)))


def _log_sigmoid(v):
    return -_softplus(-v)


def _prenorm_mod(x, vec):
    xn = x * lax.rsqrt(jnp.mean(x * x, axis=-1, keepdims=True) + EPS)
    return (xn * vec[0:1]) * (1.0 + vec[2:3]) + vec[1:2]


def _postnorm_residual(x, y, vec, weight):
    yn = y * lax.rsqrt(jnp.mean(y * y, axis=-1, keepdims=True) + EPS) * vec[3:4]
    return x + (weight * vec[4:5]) * yn


def _halves(ref):
    half = ref.shape[0] // 2
    return (slice(0, half), slice(half, 2 * half))


def _store_time_major(ref, rows, value_t):
    piece = ref.shape[2]
    first, count = rows.start // piece, (rows.stop - rows.start) // piece
    for u in range(count):
        ref[first + u] = value_t[:, u * piece:(u + 1) * piece]


def _time_major_out(channels, tm, piece, dtype, s):
    assert (tm // 2) % piece == 0 and s % tm == 0
    return (pl.BlockSpec((tm // piece, channels, piece), lambda i: (i, 0, 0)),
            jax.ShapeDtypeStruct((s // piece, channels, piece), dtype))


def _time_major_in(channels, piece):
    return pl.BlockSpec((None, channels, piece), lambda i: (i, 0, 0))


def _const_spec(shape):
    zeros = (0,) * len(shape)
    return pl.BlockSpec(shape, lambda i: zeros, pipeline_mode=pl.Buffered(1))


def _layer_spec(layer, rows, cols, col_block):
    return pl.BlockSpec((None, rows, cols), lambda i: (layer, 0, col_block), pipeline_mode=pl.Buffered(1))


def _row_spec(rows, cols):
    return pl.BlockSpec((rows, cols), lambda i: (i, 0))


def _params():
    return pltpu.CompilerParams(dimension_semantics=("arbitrary",), vmem_limit_bytes=VMEM_LIMIT)


def _ada_kernel(c_ref, w_ref, b_ref, o_ref):
    o_ref[0] = jnp.sum(_silu(c_ref[...]) * w_ref[0], axis=0, keepdims=True) + b_ref[0]


def _ada_mod(c, ada_w, ada_b):
    depth, d, n = ada_w.shape
    tn = n // 6
    out = pl.pallas_call(
        _ada_kernel,
        grid=(depth, n // tn),
        in_specs=[pl.BlockSpec((d, 1), lambda i, j: (0, 0)),
                  pl.BlockSpec((1, d, tn), lambda i, j: (i, 0, j)),
                  pl.BlockSpec((1, 1, tn), lambda i, j: (i, 0, j))],
        out_specs=pl.BlockSpec((1, 1, tn), lambda i, j: (i, 0, j)),
        out_shape=jax.ShapeDtypeStruct((depth, 1, n), F32),
        compiler_params=pltpu.CompilerParams(dimension_semantics=("arbitrary", "arbitrary"),
                                             vmem_limit_bytes=VMEM_LIMIT),
        name="ada_mod",
    )(c.reshape(d, 1), ada_w, ada_b.reshape(depth, 1, n))
    return out[:, 0, :]


def _ffn_kernel(x_ref, vec_ref, w1_ref, w3_ref, w2_ref, o_ref, h_ref, act_ref, *, weight, d_ff):
    vec = vec_ref[...]
    for rows in _halves(x_ref):
        h_ref[rows, :] = _prenorm_mod(x_ref[rows, :], vec).astype(BF16)
    for j in range(d_ff // MXU_N):
        cols = slice(j * MXU_N, (j + 1) * MXU_N)
        w1 = w1_ref[:, cols].astype(BF16)
        w3 = w3_ref[:, cols].astype(BF16)
        for rows in _halves(x_ref):
            h = h_ref[rows, :]
            act_ref[rows, cols] = (_silu(_dot(h, w1)) * _dot(h, w3)).astype(BF16)
    for rows in _halves(x_ref):
        y = _dot(act_ref[rows, :], w2_ref[...])
        o_ref[rows, :] = _postnorm_residual(x_ref[rows, :], y, vec, weight)


def _ffn_sublayer(x, vec, w1, w3, w2, layer, weight):
    s, d = x.shape
    d_ff = w1.shape[2]
    tm = min(ROW_TILE, s)
    return pl.pallas_call(
        functools.partial(_ffn_kernel, weight=weight, d_ff=d_ff),
        grid=(s // tm,),
        in_specs=[_row_spec(tm, d), _const_spec((8, d)), _layer_spec(layer, d, d_ff, 0),
                  _layer_spec(layer, d, d_ff, 0), _layer_spec(layer, d_ff, d, 0)],
        out_specs=_row_spec(tm, d),
        out_shape=jax.ShapeDtypeStruct((s, d), F32),
        scratch_shapes=[pltpu.VMEM((tm, d), BF16), pltpu.VMEM((tm, d_ff), BF16)],
        compiler_params=_params(),
        name="ffn",
    )(x, vec, w1, w3, w2)


def _ml_in_kernel(x_ref, vec_ref, wq_ref, wk_ref, wv_ref, wo_ref, wg_ref, bg_ref,
                  voq_ref, kt_ref, grow_ref, *, k_scale):
    for rows in _halves(x_ref):
        h = _prenorm_mod(x_ref[rows, :], vec_ref[...]).astype(BF16)
        base = 0
        for w_ref, act in ((wv_ref, None), (wo_ref, _sigmoid), (wq_ref, None)):
            for j in range(w_ref.shape[1] // 512):
                r = _dot(h, w_ref[:, j * 512:(j + 1) * 512])
                voq_ref[rows, base:base + 512] = (r if act is None else act(r)).astype(BF16)
                base += 512
        _store_time_major(kt_ref, rows, (_dot(h, wk_ref[...]) * k_scale).T.astype(BF16))
        _store_time_major(grow_ref, rows, (_dot(h, wg_ref[...]) + bg_ref[...]).T[0:grow_ref.shape[1], :])


def _lane_tile(a, n):
    return jnp.concatenate([a] * n, axis=1)


def _ml_core_kernel(voq_ref, kt_ref, grow_ref, nw_ref, triu_ref, x_ref, vec_ref, wout_ref,
                    out_ref, c_ref, m_ref, hs_ref, *, heads, dk, dv):
    L = CHUNK
    assert L == LANES

    @pl.when(pl.program_id(0) == 0)
    def _():
        c_ref[...] = jnp.zeros_like(c_ref)
        m_ref[...] = jnp.zeros_like(m_ref)

    row = lax.broadcasted_iota(jnp.int32, (L, L), 0)
    col = lax.broadcasted_iota(jnp.int32, (L, L), 1)
    causal = col <= row

    ones_blk = jnp.ones((L, LANES), BF16)
    sub = lax.broadcasted_iota(jnp.int32, (grow_ref.shape[0], L), 0)
    is_fg = (sub >= heads) & (sub < 2 * heads)

    dvt = heads * dv
    for c in range(voq_ref.shape[0] // L):
        t = slice(c * L, (c + 1) * L)
        gr = grow_ref[:, t]
        lf_all = jnp.where(is_fg, _log_sigmoid(gr), 0.0)
        b_all = _dot(jnp.concatenate(_split_bf16(lf_all, 3), axis=1), triu_ref[...])

        for h in range(heads):
            hv = slice(h * dv, (h + 1) * dv)
            q = voq_ref[t, 2 * dvt + h * dk:2 * dvt + (h + 1) * dk]
            kt = kt_ref[h * dk:(h + 1) * dk, t]
            v_aug = jnp.concatenate([voq_ref[t, hv], ones_blk], axis=1)
            lf_r = lf_all[heads + h:heads + h + 1, :]
            r = gr[h:h + 1, :] - b_all[heads + h:heads + h + 1, :]
            m_prev = m_ref[h:h + 1, :]

            b_c = jnp.sum(jnp.where(causal, lf_r, 0.0), axis=1, keepdims=True)
            r_mask = jnp.where(causal, r, -jnp.inf)
            g = jnp.maximum(jnp.max(r_mask, axis=1, keepdims=True), m_prev)
            s = _dot(q, kt) * jnp.exp(r_mask - g)
            inter = jnp.exp(m_prev - g)
            c_old = c_ref[h]
            numden = (_dot(s.astype(BF16), v_aug)
                      + _lane_tile(inter, dv // LANES + 1) * _dot(q, c_old.astype(BF16)))
            den = jnp.maximum(jnp.abs(numden[:, dv:]), jnp.exp(-(b_c + g)))
            hh = numden[:, :dv] * _lane_tile(1.0 / den, dv // LANES)

            g_last = g[L - 1:L, :]
            w_r = jnp.exp(r - g_last)
            decay = jnp.exp(m_prev - g_last)
            ktw = (kt.astype(F32) * w_r).astype(BF16)
            c_ref[h] = _lane_tile(decay, dv // LANES + 1) * c_old + _dot(ktw, v_aug)
            m_ref[h:h + 1, :] = jnp.sum(lf_r, axis=1, keepdims=True) + g_last

            hn = hh * lax.rsqrt(jnp.mean(hh * hh, axis=-1, keepdims=True) + EPS)
            gate = voq_ref[t, dvt + h * dv:dvt + (h + 1) * dv].astype(F32)
            hs_ref[t, hv] = (hn * nw_ref[:, hv] * gate).astype(BF16)

        y = _dot(hs_ref[t, :], wout_ref[...])
        out_ref[t, :] = _postnorm_residual(x_ref[t, :], y, vec_ref[...], 1.0)


def _mlstm_sublayer(x, vec, w_in16, layer, b_gate, norm_w, w_out16):
    s, d = x.shape
    heads = ML_HEADS
    qk = (w_in16.shape[2] - 2 * heads) // 6
    dvt = 2 * qk
    dk, dv = qk // heads, dvt // heads
    tm = min(IN_ROW_TILE, s)

    wg = jnp.pad(w_in16[layer, :, 2 * qk + 2 * dvt:], ((0, 0), (0, LANES - 2 * heads)))
    bg = jnp.pad(b_gate, (0, LANES - 2 * heads)).reshape(1, LANES)
    n_main = qk + 2 * dvt
    L = CHUNK
    blk = min(ML_CHUNKS_PER_STEP * L, s)
    kt_spec, kt_shape = _time_major_out(qk, tm, blk, BF16, s)
    g_spec, g_shape = _time_major_out(16, tm, blk, F32, s)

    voq, kt, grow = pl.pallas_call(
        functools.partial(_ml_in_kernel, k_scale=dk ** -0.5),
        grid=(s // tm,),
        in_specs=[_row_spec(tm, d), _const_spec((8, d)),
                  _layer_spec(layer, d, qk, 0), _layer_spec(layer, d, qk, 1),
                  _layer_spec(layer, d, dvt, 1), _layer_spec(layer, d, dvt, 2),
                  _const_spec((d, LANES)), _const_spec((1, LANES))],
        out_specs=[_row_spec(tm, n_main), kt_spec, g_spec],
        out_shape=[jax.ShapeDtypeStruct((s, n_main), BF16), kt_shape, g_shape],
        compiler_params=_params(),
        name="mlstm_in",
    )(x, vec, w_in16, w_in16, w_in16, w_in16, wg, bg)

    tri = jnp.arange(L)[None, :] >= jnp.arange(L)[:, None]
    triu3 = jnp.tile(tri, (3, 1)).astype(BF16)
    return pl.pallas_call(
        functools.partial(_ml_core_kernel, heads=heads, dk=dk, dv=dv),
        grid=(s // blk,),
        in_specs=[_row_spec(blk, n_main), _time_major_in(qk, blk),
                  _time_major_in(16, blk), _const_spec((1, dvt)), _const_spec((3 * L, L)),
                  _row_spec(blk, d), _const_spec((8, d)), _layer_spec(layer, dvt, d, 0)],
        out_specs=_row_spec(blk, d),
        out_shape=jax.ShapeDtypeStruct((s, d), F32),
        scratch_shapes=[pltpu.VMEM((heads, dk, dv + LANES), F32), pltpu.VMEM((8, LANES), F32),
                        pltpu.VMEM((blk, dvt), BF16)],
        compiler_params=_params(),
        name="mlstm_core",
    )(voq, kt, grow, norm_w.reshape(1, dvt), triu3, x, vec, w_out16)


def _mb_in_kernel(x_ref, vec_ref, wzx_ref, wd_ref, z_ref, xbc_ref, dtc_ref, dtr_ref):
    di = z_ref.shape[1]
    for rows in _halves(x_ref):
        h = _prenorm_mod(x_ref[rows, :], vec_ref[...]).astype(BF16)
        for j in range(di // 512):
            cols = slice(j * 512, (j + 1) * 512)
            z_ref[rows, cols] = _silu(_dot(h, wzx_ref[:, cols])).astype(BF16)
        for j in range(xbc_ref.shape[1] // 512):
            cols = slice(j * 512, (j + 1) * 512)
            xbc_ref[rows, cols] = _dot(h, wzx_ref[:, di + j * 512:di + (j + 1) * 512]).astype(BF16)
        dt = _dot(h, wd_ref[...])
        dtc_ref[rows, :] = dt
        _store_time_major(dtr_ref, rows, dt.T[0:dtr_ref.shape[1], :])


def _split_bf16(v, terms):
    parts = []
    for _ in range(terms):
        part = v.astype(BF16)
        parts.append(part)
        v = v - part.astype(F32)
    return parts


def _mb_core_kernel(z_ref, xbc_ref, dtc_ref, dtr_ref, cw_ref, cb_ref, dbc_ref, dbr_ref, ac_ref, ar_ref,
                    e_ref, dsk_ref, nw_ref, shift_ref, tril_ref, triu_ref, x_ref, vec_ref, wout_ref,
                    out_ref, xext_ref, xa_ref, st_ref, ys_ref, *, di, groups, nstate, hpg, p):
    Q = CHUNK

    @pl.when(pl.program_id(0) == 0)
    def _():
        xext_ref[Q:2 * Q, :] = jnp.zeros((Q, xext_ref.shape[1]), BF16)
        st_ref[...] = jnp.zeros_like(st_ref)

    for c in range(z_ref.shape[0] // Q):
        t = slice(c * Q, (c + 1) * Q)
        _mb_chunk(z_ref, xbc_ref, dtc_ref, dtr_ref, cw_ref, cb_ref, dbc_ref, dbr_ref, ac_ref, ar_ref,
                  e_ref, dsk_ref, nw_ref, shift_ref, tril_ref, triu_ref, xext_ref, xa_ref, st_ref, ys_ref,
                  t, di, groups, nstate, hpg, p)
        y = _dot(ys_ref[...], wout_ref[...])
        out_ref[t, :] = _postnorm_residual(x_ref[t, :], y, vec_ref[...], 1.0)


def _mb_chunk(z_ref, xbc_ref, dtc_ref, dtr_ref, cw_ref, cb_ref, dbc_ref, dbr_ref, ac_ref, ar_ref,
              e_ref, dsk_ref, nw_ref, shift_ref, tril_ref, triu_ref, xext_ref, xa_ref, st_ref, ys_ref,
              t, di, groups, nstate, hpg, p):
    Q = CHUNK
    gw = hpg * p
    conv_dim = di + 2 * groups * nstate

    xext_ref[0:Q, :] = xext_ref[Q:2 * Q, :]
    xext_ref[Q:2 * Q, :] = xbc_ref[t, :]

    def conv_silu(first_col, width):
        cols = slice(first_col, first_col + width)
        shifted = _dot(shift_ref[...], xext_ref[:, cols])
        acc = cb_ref[:, cols] + cw_ref[MB_CONV - 1:MB_CONV, cols] * xbc_ref[t, cols].astype(F32)
        for k in range(MB_CONV - 1):
            acc = acc + cw_ref[k:k + 1, cols] * shifted[k * Q:(k + 1) * Q]
        xa_ref[:, cols] = _silu(acc)

    for first_col in range(di, conv_dim, gw):
        conv_silu(first_col, gw)
    conv_silu(0, gw)

    row = lax.broadcasted_iota(jnp.int32, (Q, Q), 0)
    col = lax.broadcasted_iota(jnp.int32, (Q, Q), 1)
    causal = col <= row

    dt_c = _softplus(dtc_ref[t, :] + dbc_ref[...])
    dt_r = _softplus(dtr_ref[:, t] + dbr_ref[...])
    da_c = jnp.concatenate(_split_bf16(dt_c * -jnp.exp(ac_ref[...]), 3), axis=0)
    da_r = jnp.concatenate(_split_bf16(dt_r * -jnp.exp(ar_ref[...]), 3), axis=1)
    a_c = _dot(tril_ref[...], da_c)
    a_r = _dot(da_r, triu_ref[...])
    a_last = a_c[Q - 1:Q, :]
    per_head = jnp.concatenate([dt_c, jnp.exp(a_c), jnp.exp(a_last - a_c),
                                jnp.broadcast_to(jnp.exp(a_last), (16, LANES))], axis=0)
    per_head = jnp.concatenate(_split_bf16(per_head, 2), axis=1)
    lane = lax.broadcasted_iota(jnp.int32, (Q, LANES), 1)
    a2_c, a2_r = a_c * LOG2_E, a_r * LOG2_E

    for g in range(groups):
        gcols = slice(g * gw, (g + 1) * gw)
        if g + 1 < groups:
            conv_silu((g + 1) * gw, gw)
        per_chan = _dot(per_head, e_ref[:, gcols])
        dt_x, ea_x, dec_x = per_chan[0:Q], per_chan[Q:2 * Q], per_chan[2 * Q:3 * Q]
        eal = per_chan[3 * Q:3 * Q + 1]
        xs = xa_ref[:, gcols]
        bm = xa_ref[:, di + g * nstate:di + (g + 1) * nstate]
        cm = xa_ref[:, di + groups * nstate + g * nstate:di + groups * nstate + (g + 1) * nstate]
        cm16 = cm.astype(BF16)
        xdt = xs * dt_x
        xdt16 = xdt.astype(BF16)
        cb = jnp.where(causal, _dot_nt(cm16, bm.astype(BF16)), 0.0)

        pairs = []
        for hp in range(hpg // 2):
            x_pair = xdt16[:, hp * LANES:(hp + 1) * LANES]
            ys = []
            for e in range(2):
                hd = g * hpg + hp * 2 + e
                seg = a2_c[:, hd:hd + 1] - a2_r[hd:hd + 1, :]
                lmat = (cb * jnp.exp2(jnp.minimum(seg, 0.0))).astype(BF16)
                ys.append(_dot(lmat, x_pair))
            pairs.append(jnp.where(lane < p, ys[0], ys[1]))
        y = jnp.concatenate(pairs, axis=1)

        st = st_ref[g]
        y = y + _dot(cm16, st.astype(BF16)) * ea_x
        y = y + dsk_ref[:, gcols] * xs

        xdec = (xdt * dec_x).astype(BF16)
        st_ref[g] = eal * st + _dot(bm.T.astype(BF16), xdec)

        y = y * z_ref[t, gcols].astype(F32)
        y = y * lax.rsqrt(jnp.mean(y * y, axis=-1, keepdims=True) + EPS)
        ys_ref[:, gcols] = (y * nw_ref[:, gcols]).astype(BF16)


def _mamba_sublayer(x, vec, w_in16, layer, conv_w, conv_b, dt_bias, a_log, d_skip, norm_w, w_out16):
    s, d = x.shape
    heads = dt_bias.shape[0]
    p, groups, nstate = MB_HEADDIM, MB_GROUPS, MB_STATE
    hpg = heads // groups
    di = heads * p
    conv_dim = di + 2 * groups * nstate
    tm = min(IN_ROW_TILE, s)

    wd = jnp.pad(w_in16[layer, :, di + conv_dim:], ((0, 0), (0, LANES - heads)))
    Q = CHUNK
    blk = min(MB_CHUNKS_PER_STEP * Q, s)
    dtr_spec, dtr_shape = _time_major_out(heads, tm, blk, F32, s)

    z, xbc, dtc, dtr = pl.pallas_call(
        _mb_in_kernel,
        grid=(s // tm,),
        in_specs=[_row_spec(tm, d), _const_spec((8, d)), _layer_spec(layer, d, di + conv_dim, 0),
                  _const_spec((d, LANES))],
        out_specs=[_row_spec(tm, di), _row_spec(tm, conv_dim), _row_spec(tm, LANES), dtr_spec],
        out_shape=[jax.ShapeDtypeStruct((s, di), BF16), jax.ShapeDtypeStruct((s, conv_dim), BF16),
                   jax.ShapeDtypeStruct((s, LANES), F32), dtr_shape],
        compiler_params=_params(),
        name="mamba_in",
    )(x, vec, w_in16, wd)

    pad = (0, LANES - heads)
    head_of_chan = jnp.arange(di)[None, :] // p
    expand = jnp.tile(jnp.arange(LANES)[:, None] == head_of_chan, (2, 1)).astype(BF16)
    t_out = jnp.arange((MB_CONV - 1) * Q)
    src = Q + t_out % Q - (MB_CONV - 1) + t_out // Q
    shift = (jnp.arange(2 * Q)[None, :] == src[:, None]).astype(BF16)
    tri = jnp.arange(Q)[None, :] <= jnp.arange(Q)[:, None]
    tril3 = jnp.tile(tri, (1, 3)).astype(BF16)
    triu3 = jnp.tile(tri.T, (3, 1)).astype(BF16)
    return pl.pallas_call(
        functools.partial(_mb_core_kernel, di=di, groups=groups, nstate=nstate, hpg=hpg, p=p),
        grid=(s // blk,),
        in_specs=[_row_spec(blk, di), _row_spec(blk, conv_dim), _row_spec(blk, LANES),
                  _time_major_in(heads, blk),
                  _const_spec((MB_CONV, conv_dim)), _const_spec((1, conv_dim)),
                  _const_spec((1, LANES)), _const_spec((heads, 1)),
                  _const_spec((1, LANES)), _const_spec((heads, 1)),
                  _const_spec((2 * LANES, di)), _const_spec((1, di)), _const_spec((1, di)),
                  _const_spec(((MB_CONV - 1) * Q, 2 * Q)), _const_spec((Q, 3 * Q)), _const_spec((3 * Q, Q)),
                  _row_spec(blk, d), _const_spec((8, d)), _layer_spec(layer, di, d, 0)],
        out_specs=_row_spec(blk, d),
        out_shape=jax.ShapeDtypeStruct((s, d), F32),
        scratch_shapes=[pltpu.VMEM((2 * Q, conv_dim), BF16), pltpu.VMEM((Q, conv_dim), F32),
                        pltpu.VMEM((groups, nstate, hpg * p), F32), pltpu.VMEM((Q, di), BF16)],
        compiler_params=_params(),
        name="mamba_core",
    )(z, xbc, dtc, dtr, conv_w, conv_b.reshape(1, conv_dim),
      jnp.pad(dt_bias, pad).reshape(1, LANES), dt_bias.reshape(heads, 1),
      jnp.pad(a_log, pad).reshape(1, LANES), a_log.reshape(heads, 1),
      expand, jnp.repeat(d_skip, p).reshape(1, di), norm_w.reshape(1, di), shift, tril3, triu3,
      x, vec, w_out16)


def kernel(x, c, ada_w, ada_b, norm_pre, norm_post, ffn_w1, ffn_w3, ffn_w2, ml_w_in, ml_b_gate, ml_norm_w,
           ml_w_out, mb_w_in, mb_conv_w, mb_conv_b, mb_dt_bias, mb_A_log, mb_D, mb_norm_w, mb_w_out):
    batch, seq, d = x.shape
    assert batch == 1, "the kernels assume one sequence"
    depth = ada_w.shape[0]
    mod = _ada_mod(c, ada_w, ada_b).reshape(depth, N_SUB, 3, d)
    pad_rows = jnp.zeros((3, d), F32)
    d_ff = ffn_w1.shape[-1]
    w1 = ffn_w1.reshape(depth * 2, d, d_ff)
    w3 = ffn_w3.reshape(depth * 2, d, d_ff)
    w2 = ffn_w2.astype(BF16).reshape(depth * 2, d_ff, d)
    ml_in16, ml_out16 = ml_w_in.astype(BF16), ml_w_out.astype(BF16)
    mb_in16, mb_out16 = mb_w_in.astype(BF16), mb_w_out.astype(BF16)
    h = x.reshape(seq, d)
    for i in range(depth):
        j = i // 2
        for sub in range(N_SUB):
            vec = jnp.concatenate([norm_pre[i, sub][None], mod[i, sub, 0][None], mod[i, sub, 1][None],
                                   norm_post[i, sub][None], mod[i, sub, 2][None], pad_rows], axis=0)
            if sub != 1:
                h = _ffn_sublayer(h, vec, w1, w3, w2, 2 * i + sub // 2, FFN_HALF)
            elif i % 2 == 0:
                h = _mlstm_sublayer(h, vec, ml_in16, j, ml_b_gate[j], ml_norm_w[j], ml_out16)
            else:
                h = _mamba_sublayer(h, vec, mb_in16, j, mb_conv_w[j], mb_conv_b[j], mb_dt_bias[j],
                                    mb_A_log[j], mb_D[j], mb_norm_w[j], mb_out16)
    return h.reshape(batch, seq, d)
```

```python
import functools

import jax
import jax.numpy as jnp
from jax import lax
from jax.experimental import pallas as pl
from jax.experimental.pallas import tpu as pltpu

F32 = jnp.float32
BF16 = jnp.bfloat16
HIGHEST = lax.Precision.HIGHEST
EPS = 1e-6
LOG2_E = 1.4426950408889634

N_SUB = 3
FFN_HALF = 0.5
ML_HEADS = 4
MB_HEADDIM = 64
MB_GROUPS = 4
MB_STATE = 128
MB_CONV = 4

LANES = 128
MXU_N = 256
ROW_TILE = 512
IN_ROW_TILE = 1024
CHUNK = 128
ML_CHUNKS_PER_STEP = 4
MB_CHUNKS_PER_STEP = 4
VMEM_LIMIT = 56 * 1024 * 1024

_NT = (((1,), (1,)), ((), ()))


def _dot(a, b, precision=None):
    return jnp.dot(a, b, preferred_element_type=F32, precision=precision)


def _dot_nt(a, b):
    return lax.dot_general(a, b, _NT, preferred_element_type=F32)


def _sigmoid(v):
    return 0.5 + 0.5 * jnp.tanh(0.5 * v)


def _silu(v):
    half = 0.5 * v
    return half + half * jnp.tanh(half)


def _softplus(v):
    return jnp.maximum(v, 0.0) + jnp.log(1.0 + jnp.exp(-jnp.abs(v)))


def _log_sigmoid(v):
    return -_softplus(-v)


def _prenorm_mod(x, vec):
    xn = x * lax.rsqrt(jnp.mean(x * x, axis=-1, keepdims=True) + EPS)
    return (xn * vec[0:1]) * (1.0 + vec[2:3]) + vec[1:2]


def _postnorm_residual(x, y, vec, weight):
    yn = y * lax.rsqrt(jnp.mean(y * y, axis=-1, keepdims=True) + EPS) * vec[3:4]
    return x + (weight * vec[4:5]) * yn


def _halves(ref):
    half = ref.shape[0] // 2
    return (slice(0, half), slice(half, 2 * half))


def _store_time_major(ref, rows, value_t):
    piece = ref.shape[2]
    first, count = rows.start // piece, (rows.stop - rows.start) // piece
    for u in range(count):
        ref[first + u] = value_t[:, u * piece:(u + 1) * piece]


def _time_major_out(channels, tm, piece, dtype, s):
    assert (tm // 2) % piece == 0 and s % tm == 0
    return (pl.BlockSpec((tm // piece, channels, piece), lambda i: (i, 0, 0)),
            jax.ShapeDtypeStruct((s // piece, channels, piece), dtype))


def _time_major_in(channels, piece):
    return pl.BlockSpec((None, channels, piece), lambda i: (i, 0, 0))


def _const_spec(shape):
    zeros = (0,) * len(shape)
    return pl.BlockSpec(shape, lambda i: zeros, pipeline_mode=pl.Buffered(1))


def _layer_spec(layer, rows, cols, col_block):
    return pl.BlockSpec((None, rows, cols), lambda i: (layer, 0, col_block), pipeline_mode=pl.Buffered(1))


def _row_spec(rows, cols):
    return pl.BlockSpec((rows, cols), lambda i: (i, 0))


def _params():
    return pltpu.CompilerParams(dimension_semantics=("arbitrary",), vmem_limit_bytes=VMEM_LIMIT)


def _ada_kernel(c_ref, w_ref, b_ref, o_ref):
    o_ref[0] = jnp.sum(_silu(c_ref[...]) * w_ref[0], axis=0, keepdims=True) + b_ref[0]


def _ada_mod(c, ada_w, ada_b):
    depth, d, n = ada_w.shape
    tn = n // 6
    out = pl.pallas_call(
        _ada_kernel,
        grid=(depth, n // tn),
        in_specs=[pl.BlockSpec((d, 1), lambda i, j: (0, 0)),
                  pl.BlockSpec((1, d, tn), lambda i, j: (i, 0, j)),
                  pl.BlockSpec((1, 1, tn), lambda i, j: (i, 0, j))],
        out_specs=pl.BlockSpec((1, 1, tn), lambda i, j: (i, 0, j)),
        out_shape=jax.ShapeDtypeStruct((depth, 1, n), F32),
        compiler_params=pltpu.CompilerParams(dimension_semantics=("arbitrary", "arbitrary"),
                                             vmem_limit_bytes=VMEM_LIMIT),
        name="ada_mod",
    )(c.reshape(d, 1), ada_w, ada_b.reshape(depth, 1, n))
    return out[:, 0, :]


def _ffn_kernel(x_ref, vec_ref, w1_ref, w3_ref, w2_ref, o_ref, h_ref, act_ref, *, weight, d_ff):
    vec = vec_ref[...]
    for rows in _halves(x_ref):
        h_ref[rows, :] = _prenorm_mod(x_ref[rows, :], vec).astype(BF16)
    for j in range(d_ff // MXU_N):
        cols = slice(j * MXU_N, (j + 1) * MXU_N)
        w1 = w1_ref[:, cols].astype(BF16)
        w3 = w3_ref[:, cols].astype(BF16)
        for rows in _halves(x_ref):
            h = h_ref[rows, :]
            act_ref[rows, cols] = (_silu(_dot(h, w1)) * _dot(h, w3)).astype(BF16)
    for rows in _halves(x_ref):
        y = _dot(act_ref[rows, :], w2_ref[...])
        o_ref[rows, :] = _postnorm_residual(x_ref[rows, :], y, vec, weight)


def _ffn_sublayer(x, vec, w1, w3, w2, layer, weight):
    s, d = x.shape
    d_ff = w1.shape[2]
    tm = min(ROW_TILE, s)
    return pl.pallas_call(
        functools.partial(_ffn_kernel, weight=weight, d_ff=d_ff),
        grid=(s // tm,),
        in_specs=[_row_spec(tm, d), _const_spec((8, d)), _layer_spec(layer, d, d_ff, 0),
                  _layer_spec(layer, d, d_ff, 0), _layer_spec(layer, d_ff, d, 0)],
        out_specs=_row_spec(tm, d),
        out_shape=jax.ShapeDtypeStruct((s, d), F32),
        scratch_shapes=[pltpu.VMEM((tm, d), BF16), pltpu.VMEM((tm, d_ff), BF16)],
        compiler_params=_params(),
        name="ffn",
    )(x, vec, w1, w3, w2)


def _ml_in_kernel(x_ref, vec_ref, wq_ref, wk_ref, wv_ref, wo_ref, wg_ref, bg_ref,
                  voq_ref, kt_ref, grow_ref, *, k_scale):
    for rows in _halves(x_ref):
        h = _prenorm_mod(x_ref[rows, :], vec_ref[...]).astype(BF16)
        base = 0
        for w_ref, act in ((wv_ref, None), (wo_ref, _sigmoid), (wq_ref, None)):
            for j in range(w_ref.shape[1] // 512):
                r = _dot(h, w_ref[:, j * 512:(j + 1) * 512])
                voq_ref[rows, base:base + 512] = (r if act is None else act(r)).astype(BF16)
                base += 512
        _store_time_major(kt_ref, rows, (_dot(h, wk_ref[...]) * k_scale).T.astype(BF16))
        _store_time_major(grow_ref, rows, (_dot(h, wg_ref[...]) + bg_ref[...]).T[0:grow_ref.shape[1], :])


def _lane_tile(a, n):
    return jnp.concatenate([a] * n, axis=1)


def _ml_core_kernel(voq_ref, kt_ref, grow_ref, nw_ref, triu_ref, x_ref, vec_ref, wout_ref,
                    out_ref, c_ref, m_ref, hs_ref, *, heads, dk, dv):
    L = CHUNK
    assert L == LANES

    @pl.when(pl.program_id(0) == 0)
    def _():
        c_ref[...] = jnp.zeros_like(c_ref)
        m_ref[...] = jnp.zeros_like(m_ref)

    row = lax.broadcasted_iota(jnp.int32, (L, L), 0)
    col = lax.broadcasted_iota(jnp.int32, (L, L), 1)
    causal = col <= row

    ones_blk = jnp.ones((L, LANES), BF16)
    sub = lax.broadcasted_iota(jnp.int32, (grow_ref.shape[0], L), 0)
    is_fg = (sub >= heads) & (sub < 2 * heads)

    dvt = heads * dv
    for c in range(voq_ref.shape[0] // L):
        t = slice(c * L, (c + 1) * L)
        gr = grow_ref[:, t]
        lf_all = jnp.where(is_fg, _log_sigmoid(gr), 0.0)
        b_all = _dot(jnp.concatenate(_split_bf16(lf_all, 3), axis=1), triu_ref[...])

        for h in range(heads):
            hv = slice(h * dv, (h + 1) * dv)
            q = voq_ref[t, 2 * dvt + h * dk:2 * dvt + (h + 1) * dk]
            kt = kt_ref[h * dk:(h + 1) * dk, t]
            v_aug = jnp.concatenate([voq_ref[t, hv], ones_blk], axis=1)
            lf_r = lf_all[heads + h:heads + h + 1, :]
            r = gr[h:h + 1, :] - b_all[heads + h:heads + h + 1, :]
            m_prev = m_ref[h:h + 1, :]

            b_c = jnp.sum(jnp.where(causal, lf_r, 0.0), axis=1, keepdims=True)
            r_mask = jnp.where(causal, r, -jnp.inf)
            g = jnp.maximum(jnp.max(r_mask, axis=1, keepdims=True), m_prev)
            s = _dot(q, kt) * jnp.exp(r_mask - g)
            inter = jnp.exp(m_prev - g)
            c_old = c_ref[h]
            numden = (_dot(s.astype(BF16), v_aug)
                      + _lane_tile(inter, dv // LANES + 1) * _dot(q, c_old.astype(BF16)))
            den = jnp.maximum(jnp.abs(numden[:, dv:]), jnp.exp(-(b_c + g)))
            hh = numden[:, :dv] * _lane_tile(1.0 / den, dv // LANES)

            g_last = g[L - 1:L, :]
            w_r = jnp.exp(r - g_last)
            decay = jnp.exp(m_prev - g_last)
            ktw = (kt.astype(F32) * w_r).astype(BF16)
            c_ref[h] = _lane_tile(decay, dv // LANES + 1) * c_old + _dot(ktw, v_aug)
            m_ref[h:h + 1, :] = jnp.sum(lf_r, axis=1, keepdims=True) + g_last

            hn = hh * lax.rsqrt(jnp.mean(hh * hh, axis=-1, keepdims=True) + EPS)
            gate = voq_ref[t, dvt + h * dv:dvt + (h + 1) * dv].astype(F32)
            hs_ref[t, hv] = (hn * nw_ref[:, hv] * gate).astype(BF16)

    for rows in _halves(x_ref):
        y = _dot(hs_ref[rows, :], wout_ref[...])
        out_ref[rows, :] = _postnorm_residual(x_ref[rows, :], y, vec_ref[...], 1.0)


def _mlstm_sublayer(x, vec, w_in16, layer, b_gate, norm_w, w_out16):
    s, d = x.shape
    heads = ML_HEADS
    qk = (w_in16.shape[2] - 2 * heads) // 6
    dvt = 2 * qk
    dk, dv = qk // heads, dvt // heads
    tm = min(IN_ROW_TILE, s)

    wg = jnp.pad(w_in16[layer, :, 2 * qk + 2 * dvt:], ((0, 0), (0, LANES - 2 * heads)))
    bg = jnp.pad(b_gate, (0, LANES - 2 * heads)).reshape(1, LANES)
    n_main = qk + 2 * dvt
    L = CHUNK
    blk = min(ML_CHUNKS_PER_STEP * L, s)
    kt_spec, kt_shape = _time_major_out(qk, tm, blk, BF16, s)
    g_spec, g_shape = _time_major_out(16, tm, blk, F32, s)

    voq, kt, grow = pl.pallas_call(
        functools.partial(_ml_in_kernel, k_scale=dk ** -0.5),
        grid=(s // tm,),
        in_specs=[_row_spec(tm, d), _const_spec((8, d)),
                  _layer_spec(layer, d, qk, 0), _layer_spec(layer, d, qk, 1),
                  _layer_spec(layer, d, dvt, 1), _layer_spec(layer, d, dvt, 2),
                  _const_spec((d, LANES)), _const_spec((1, LANES))],
        out_specs=[_row_spec(tm, n_main), kt_spec, g_spec],
        out_shape=[jax.ShapeDtypeStruct((s, n_main), BF16), kt_shape, g_shape],
        compiler_params=_params(),
        name="mlstm_in",
    )(x, vec, w_in16, w_in16, w_in16, w_in16, wg, bg)

    tri = jnp.arange(L)[None, :] >= jnp.arange(L)[:, None]
    triu3 = jnp.tile(tri, (3, 1)).astype(BF16)
    return pl.pallas_call(
        functools.partial(_ml_core_kernel, heads=heads, dk=dk, dv=dv),
        grid=(s // blk,),
        in_specs=[_row_spec(blk, n_main), _time_major_in(qk, blk),
                  _time_major_in(16, blk), _const_spec((1, dvt)), _const_spec((3 * L, L)),
                  _row_spec(blk, d), _const_spec((8, d)), _layer_spec(layer, dvt, d, 0)],
        out_specs=_row_spec(blk, d),
        out_shape=jax.ShapeDtypeStruct((s, d), F32),
        scratch_shapes=[pltpu.VMEM((heads, dk, dv + LANES), F32), pltpu.VMEM((8, LANES), F32),
                        pltpu.VMEM((blk, dvt), BF16)],
        compiler_params=_params(),
        name="mlstm_core",
    )(voq, kt, grow, norm_w.reshape(1, dvt), triu3, x, vec, w_out16)


def _mb_in_kernel(x_ref, vec_ref, wzx_ref, wd_ref, z_ref, xbc_ref, dtc_ref, dtr_ref):
    di = z_ref.shape[1]
    for rows in _halves(x_ref):
        h = _prenorm_mod(x_ref[rows, :], vec_ref[...]).astype(BF16)
        for j in range(di // 512):
            cols = slice(j * 512, (j + 1) * 512)
            z_ref[rows, cols] = _silu(_dot(h, wzx_ref[:, cols])).astype(BF16)
        for j in range(xbc_ref.shape[1] // 512):
            cols = slice(j * 512, (j + 1) * 512)
            xbc_ref[rows, cols] = _dot(h, wzx_ref[:, di + j * 512:di + (j + 1) * 512]).astype(BF16)
        dt = _dot(h, wd_ref[...])
        dtc_ref[rows, :] = dt
        _store_time_major(dtr_ref, rows, dt.T[0:dtr_ref.shape[1], :])


def _split_bf16(v, terms):
    parts = []
    for _ in range(terms):
        part = v.astype(BF16)
        parts.append(part)
        v = v - part.astype(F32)
    return parts


def _mb_core_kernel(z_ref, xbc_ref, dtc_ref, dtr_ref, cw_ref, cb_ref, dbc_ref, dbr_ref, ac_ref, ar_ref,
                    e_ref, dsk_ref, nw_ref, shift_ref, tril_ref, triu_ref, x_ref, vec_ref, wout_ref,
                    out_ref, xext_ref, xa_ref, st_ref, ys_ref, *, di, groups, nstate, hpg, p):
    Q = CHUNK

    @pl.when(pl.program_id(0) == 0)
    def _():
        xext_ref[Q:2 * Q, :] = jnp.zeros((Q, xext_ref.shape[1]), BF16)
        st_ref[...] = jnp.zeros_like(st_ref)

    for c in range(z_ref.shape[0] // Q):
        t = slice(c * Q, (c + 1) * Q)
        _mb_chunk(z_ref, xbc_ref, dtc_ref, dtr_ref, cw_ref, cb_ref, dbc_ref, dbr_ref, ac_ref, ar_ref,
                  e_ref, dsk_ref, nw_ref, shift_ref, tril_ref, triu_ref, xext_ref, xa_ref, st_ref, ys_ref,
                  t, di, groups, nstate, hpg, p)

    for rows in _halves(x_ref):
        y = _dot(ys_ref[rows, :], wout_ref[...])
        out_ref[rows, :] = _postnorm_residual(x_ref[rows, :], y, vec_ref[...], 1.0)


def _mb_chunk(z_ref, xbc_ref, dtc_ref, dtr_ref, cw_ref, cb_ref, dbc_ref, dbr_ref, ac_ref, ar_ref,
              e_ref, dsk_ref, nw_ref, shift_ref, tril_ref, triu_ref, xext_ref, xa_ref, st_ref, ys_ref,
              t, di, groups, nstate, hpg, p):
    Q = CHUNK
    gw = hpg * p
    conv_dim = di + 2 * groups * nstate

    xext_ref[0:Q, :] = xext_ref[Q:2 * Q, :]
    xext_ref[Q:2 * Q, :] = xbc_ref[t, :]

    def conv_silu(first_col, width):
        cols = slice(first_col, first_col + width)
        shifted = _dot(shift_ref[...], xext_ref[:, cols])
        acc = cb_ref[:, cols] + cw_ref[MB_CONV - 1:MB_CONV, cols] * xbc_ref[t, cols].astype(F32)
        for k in range(MB_CONV - 1):
            acc = acc + cw_ref[k:k + 1, cols] * shifted[k * Q:(k + 1) * Q]
        xa_ref[:, cols] = _silu(acc)

    for first_col in range(di, conv_dim, gw):
        conv_silu(first_col, gw)
    conv_silu(0, gw)

    row = lax.broadcasted_iota(jnp.int32, (Q, Q), 0)
    col = lax.broadcasted_iota(jnp.int32, (Q, Q), 1)
    causal = col <= row

    dt_c = _softplus(dtc_ref[t, :] + dbc_ref[...])
    dt_r = _softplus(dtr_ref[:, t] + dbr_ref[...])
    da_c = jnp.concatenate(_split_bf16(dt_c * -jnp.exp(ac_ref[...]), 3), axis=0)
    da_r = jnp.concatenate(_split_bf16(dt_r * -jnp.exp(ar_ref[...]), 3), axis=1)
    a_c = _dot(tril_ref[...], da_c)
    a_r = _dot(da_r, triu_ref[...])
    a_last = a_c[Q - 1:Q, :]
    per_head = jnp.concatenate([dt_c, jnp.exp(a_c), jnp.exp(a_last - a_c),
                                jnp.broadcast_to(jnp.exp(a_last), (16, LANES))], axis=0)
    per_head = jnp.concatenate(_split_bf16(per_head, 2), axis=1)
    lane = lax.broadcasted_iota(jnp.int32, (Q, LANES), 1)
    a2_c, a2_r = a_c * LOG2_E, a_r * LOG2_E

    for g in range(groups):
        gcols = slice(g * gw, (g + 1) * gw)
        if g + 1 < groups:
            conv_silu((g + 1) * gw, gw)
        per_chan = _dot(per_head, e_ref[:, gcols])
        dt_x, ea_x, dec_x = per_chan[0:Q], per_chan[Q:2 * Q], per_chan[2 * Q:3 * Q]
        eal = per_chan[3 * Q:3 * Q + 1]
        xs = xa_ref[:, gcols]
        bm = xa_ref[:, di + g * nstate:di + (g + 1) * nstate]
        cm = xa_ref[:, di + groups * nstate + g * nstate:di + groups * nstate + (g + 1) * nstate]
        cm16 = cm.astype(BF16)
        xdt = xs * dt_x
        xdt16 = xdt.astype(BF16)
        cb = jnp.where(causal, _dot_nt(cm16, bm.astype(BF16)), 0.0)

        pairs = []
        for hp in range(hpg // 2):
            x_pair = xdt16[:, hp * LANES:(hp + 1) * LANES]
            ys = []
            for e in range(2):
                hd = g * hpg + hp * 2 + e
                seg = a2_c[:, hd:hd + 1] - a2_r[hd:hd + 1, :]
                lmat = (cb * jnp.exp2(jnp.minimum(seg, 0.0))).astype(BF16)
                ys.append(_dot(lmat, x_pair))
            pairs.append(jnp.where(lane < p, ys[0], ys[1]))
        y = jnp.concatenate(pairs, axis=1)

        st = st_ref[g]
        y = y + _dot(cm16, st.astype(BF16)) * ea_x
        y = y + dsk_ref[:, gcols] * xs

        xdec = (xdt * dec_x).astype(BF16)
        st_ref[g] = eal * st + _dot(bm.T.astype(BF16), xdec)

        y = y * z_ref[t, gcols].astype(F32)
        y = y * lax.rsqrt(jnp.mean(y * y, axis=-1, keepdims=True) + EPS)
        ys_ref[t, gcols] = (y * nw_ref[:, gcols]).astype(BF16)


def _mamba_sublayer(x, vec, w_in16, layer, conv_w, conv_b, dt_bias, a_log, d_skip, norm_w, w_out16):
    s, d = x.shape
    heads = dt_bias.shape[0]
    p, groups, nstate = MB_HEADDIM, MB_GROUPS, MB_STATE
    hpg = heads // groups
    di = heads * p
    conv_dim = di + 2 * groups * nstate
    tm = min(IN_ROW_TILE, s)

    wd = jnp.pad(w_in16[layer, :, di + conv_dim:], ((0, 0), (0, LANES - heads)))
    Q = CHUNK
    blk = min(MB_CHUNKS_PER_STEP * Q, s)
    dtr_spec, dtr_shape = _time_major_out(heads, tm, blk, F32, s)

    z, xbc, dtc, dtr = pl.pallas_call(
        _mb_in_kernel,
        grid=(s // tm,),
        in_specs=[_row_spec(tm, d), _const_spec((8, d)), _layer_spec(layer, d, di + conv_dim, 0),
                  _const_spec((d, LANES))],
        out_specs=[_row_spec(tm, di), _row_spec(tm, conv_dim), _row_spec(tm, LANES), dtr_spec],
        out_shape=[jax.ShapeDtypeStruct((s, di), BF16), jax.ShapeDtypeStruct((s, conv_dim), BF16),
                   jax.ShapeDtypeStruct((s, LANES), F32), dtr_shape],
        compiler_params=_params(),
        name="mamba_in",
    )(x, vec, w_in16, wd)

    pad = (0, LANES - heads)
    head_of_chan = jnp.arange(di)[None, :] // p
    expand = jnp.tile(jnp.arange(LANES)[:, None] == head_of_chan, (2, 1)).astype(BF16)
    t_out = jnp.arange((MB_CONV - 1) * Q)
    src = Q + t_out % Q - (MB_CONV - 1) + t_out // Q
    shift = (jnp.arange(2 * Q)[None, :] == src[:, None]).astype(BF16)
    tri = jnp.arange(Q)[None, :] <= jnp.arange(Q)[:, None]
    tril3 = jnp.tile(tri, (1, 3)).astype(BF16)
    triu3 = jnp.tile(tri.T, (3, 1)).astype(BF16)
    return pl.pallas_call(
        functools.partial(_mb_core_kernel, di=di, groups=groups, nstate=nstate, hpg=hpg, p=p),
        grid=(s // blk,),
        in_specs=[_row_spec(blk, di), _row_spec(blk, conv_dim), _row_spec(blk, LANES),
                  _time_major_in(heads, blk),
                  _const_spec((MB_CONV, conv_dim)), _const_spec((1, conv_dim)),
                  _const_spec((1, LANES)), _const_spec((heads, 1)),
                  _const_spec((1, LANES)), _const_spec((heads, 1)),
                  _const_spec((2 * LANES, di)), _const_spec((1, di)), _const_spec((1, di)),
                  _const_spec(((MB_CONV - 1) * Q, 2 * Q)), _const_spec((Q, 3 * Q)), _const_spec((3 * Q, Q)),
                  _row_spec(blk, d), _const_spec((8, d)), _layer_spec(layer, di, d, 0)],
        out_specs=_row_spec(blk, d),
        out_shape=jax.ShapeDtypeStruct((s, d), F32),
        scratch_shapes=[pltpu.VMEM((2 * Q, conv_dim), BF16), pltpu.VMEM((Q, conv_dim), F32),
                        pltpu.VMEM((groups, nstate, hpg * p), F32), pltpu.VMEM((blk, di), BF16)],
        compiler_params=_params(),
        name="mamba_core",
    )(z, xbc, dtc, dtr, conv_w, conv_b.reshape(1, conv_dim),
      jnp.pad(dt_bias, pad).reshape(1, LANES), dt_bias.reshape(heads, 1),
      jnp.pad(a_log, pad).reshape(1, LANES), a_log.reshape(heads, 1),
      expand, jnp.repeat(d_skip, p).reshape(1, di), norm_w.reshape(1, di), shift, tril3, triu3,
      x, vec, w_out16)


def kernel(x, c, ada_w, ada_b, norm_pre, norm_post, ffn_w1, ffn_w3, ffn_w2, ml_w_in, ml_b_gate, ml_norm_w,
           ml_w_out, mb_w_in, mb_conv_w, mb_conv_b, mb_dt_bias, mb_A_log, mb_D, mb_norm_w, mb_w_out):
    batch, seq, d = x.shape
    assert batch == 1, "the kernels assume one sequence"
    depth = ada_w.shape[0]
    mod = _ada_mod(c, ada_w, ada_b).reshape(depth, N_SUB, 3, d)
    pad_rows = jnp.zeros((3, d), F32)
    d_ff = ffn_w1.shape[-1]
    w1 = ffn_w1.reshape(depth * 2, d, d_ff)
    w3 = ffn_w3.reshape(depth * 2, d, d_ff)
    w2 = ffn_w2.astype(BF16).reshape(depth * 2, d_ff, d)
    ml_in16, ml_out16 = ml_w_in.astype(BF16), ml_w_out.astype(BF16)
    mb_in16, mb_out16 = mb_w_in.astype(BF16), mb_w_out.astype(BF16)
    h = x.reshape(seq, d)
    for i in range(depth):
        j = i // 2
        for sub in range(N_SUB):
            vec = jnp.concatenate([norm_pre[i, sub][None], mod[i, sub, 0][None], mod[i, sub, 1][None],
                                   norm_post[i, sub][None], mod[i, sub, 2][None], pad_rows], axis=0)
            if sub != 1:
                h = _ffn_sublayer(h, vec, w1, w3, w2, 2 * i + sub // 2, FFN_HALF)
            elif i % 2 == 0:
                h = _mlstm_sublayer(h, vec, ml_in16, j, ml_b_gate[j], ml_norm_w[j], ml_out16)
            else:
                h = _mamba_sublayer(h, vec, mb_in16, j, mb_conv_w[j], mb_conv_b[j], mb_dt_bias[j],
                                    mb_A_log[j], mb_D[j], mb_norm_w[j], mb_out16)
    return h.reshape(batch, seq, d)
```

```python
import functools

import jax
import jax.numpy as jnp
from jax import lax
from jax.experimental import pallas as pl
from jax.experimental.pallas import tpu as pltpu

F32 = jnp.float32
BF16 = jnp.bfloat16
HIGHEST = lax.Precision.HIGHEST
EPS = 1e-6
LOG2_E = 1.4426950408889634

N_SUB = 3
FFN_HALF = 0.5
ML_HEADS = 4
MB_HEADDIM = 64
MB_GROUPS = 4
MB_STATE = 128
MB_CONV = 4

LANES = 128
MXU_N = 256
ROW_TILE = 512
IN_ROW_TILE = 1024
CHUNK = 128
ML_CHUNKS_PER_STEP = 4
MB_CHUNKS_PER_STEP = 4
VMEM_LIMIT = 56 * 1024 * 1024

_NT = (((1,), (1,)), ((), ()))


def _dot(a, b, precision=None):
    return jnp.dot(a, b, preferred_element_type=F32, precision=precision)


def _dot_nt(a, b):
    return lax.dot_general(a, b, _NT, preferred_element_type=F32)


def _sigmoid(v):
    return 0.5 + 0.5 * jnp.tanh(0.5 * v)


def _silu(v):
    half = 0.5 * v
    return half + half * jnp.tanh(half)


def _softplus(v):
    return jnp.maximum(v, 0.0) + jnp.log(1.0 + jnp.exp(-jnp.abs(v)))


def _log_sigmoid(v):
    return -_softplus(-v)


def _prenorm_mod(x, vec):
    xn = x * lax.rsqrt(jnp.mean(x * x, axis=-1, keepdims=True) + EPS)
    return (xn * vec[0:1]) * (1.0 + vec[2:3]) + vec[1:2]


def _postnorm_residual(x, y, vec, weight):
    yn = y * lax.rsqrt(jnp.mean(y * y, axis=-1, keepdims=True) + EPS) * vec[3:4]
    return x + (weight * vec[4:5]) * yn


def _halves(ref):
    half = ref.shape[0] // 2
    return (slice(0, half), slice(half, 2 * half))


def _store_time_major(ref, rows, value_t):
    piece = ref.shape[2]
    first, count = rows.start // piece, (rows.stop - rows.start) // piece
    for u in range(count):
        ref[first + u] = value_t[:, u * piece:(u + 1) * piece]


def _time_major_out(channels, tm, piece, dtype, s):
    assert (tm // 2) % piece == 0 and s % tm == 0
    return (pl.BlockSpec((tm // piece, channels, piece), lambda i: (i, 0, 0)),
            jax.ShapeDtypeStruct((s // piece, channels, piece), dtype))


def _time_major_in(channels, piece):
    return pl.BlockSpec((None, channels, piece), lambda i: (i, 0, 0))


def _const_spec(shape):
    zeros = (0,) * len(shape)
    return pl.BlockSpec(shape, lambda i: zeros, pipeline_mode=pl.Buffered(1))


def _layer_spec(layer, rows, cols, col_block):
    return pl.BlockSpec((None, rows, cols), lambda i: (layer, 0, col_block), pipeline_mode=pl.Buffered(1))


def _row_spec(rows, cols):
    return pl.BlockSpec((rows, cols), lambda i: (i, 0))


def _params():
    return pltpu.CompilerParams(dimension_semantics=("arbitrary",), vmem_limit_bytes=VMEM_LIMIT)


def _ada_kernel(c_ref, w_ref, b_ref, o_ref):
    o_ref[0] = jnp.sum(_silu(c_ref[...]) * w_ref[0], axis=0, keepdims=True) + b_ref[0]


def _ada_mod(c, ada_w, ada_b):
    depth, d, n = ada_w.shape
    tn = n // 6
    out = pl.pallas_call(
        _ada_kernel,
        grid=(depth, n // tn),
        in_specs=[pl.BlockSpec((d, 1), lambda i, j: (0, 0)),
                  pl.BlockSpec((1, d, tn), lambda i, j: (i, 0, j)),
                  pl.BlockSpec((1, 1, tn), lambda i, j: (i, 0, j))],
        out_specs=pl.BlockSpec((1, 1, tn), lambda i, j: (i, 0, j)),
        out_shape=jax.ShapeDtypeStruct((depth, 1, n), F32),
        compiler_params=pltpu.CompilerParams(dimension_semantics=("arbitrary", "arbitrary"),
                                             vmem_limit_bytes=VMEM_LIMIT),
        name="ada_mod",
    )(c.reshape(d, 1), ada_w, ada_b.reshape(depth, 1, n))
    return out[:, 0, :]


def _ffn_kernel(x_ref, vec_ref, w1_ref, w3_ref, w2_ref, o_ref, h_ref, act_ref, *, weight, d_ff):
    vec = vec_ref[...]
    for rows in _halves(x_ref):
        h_ref[rows, :] = _prenorm_mod(x_ref[rows, :], vec).astype(BF16)
    for j in range(d_ff // MXU_N):
        cols = slice(j * MXU_N, (j + 1) * MXU_N)
        w1 = w1_ref[:, cols].astype(BF16)
        w3 = w3_ref[:, cols].astype(BF16)
        for rows in _halves(x_ref):
            h = h_ref[rows, :]
            act_ref[rows, cols] = (_silu(_dot(h, w1)) * _dot(h, w3)).astype(BF16)
    for rows in _halves(x_ref):
        y = _dot(act_ref[rows, :], w2_ref[...])
        o_ref[rows, :] = _postnorm_residual(x_ref[rows, :], y, vec, weight)


def _ffn_sublayer(x, vec, w1, w3, w2, layer, weight):
    s, d = x.shape
    d_ff = w1.shape[2]
    tm = min(ROW_TILE, s)
    return pl.pallas_call(
        functools.partial(_ffn_kernel, weight=weight, d_ff=d_ff),
        grid=(s // tm,),
        in_specs=[_row_spec(tm, d), _const_spec((8, d)), _layer_spec(layer, d, d_ff, 0),
                  _layer_spec(layer, d, d_ff, 0), _layer_spec(layer, d_ff, d, 0)],
        out_specs=_row_spec(tm, d),
        out_shape=jax.ShapeDtypeStruct((s, d), F32),
        scratch_shapes=[pltpu.VMEM((tm, d), BF16), pltpu.VMEM((tm, d_ff), BF16)],
        compiler_params=_params(),
        name="ffn",
    )(x, vec, w1, w3, w2)


def _ml_in_kernel(x_ref, vec_ref, wq_ref, wk_ref, wv_ref, wo_ref, wg_ref, bg_ref,
                  voq_ref, kt_ref, grow_ref, *, k_scale):
    for rows in _halves(x_ref):
        h = _prenorm_mod(x_ref[rows, :], vec_ref[...]).astype(BF16)
        base = 0
        for w_ref, act in ((wv_ref, None), (wo_ref, _sigmoid), (wq_ref, None)):
            for j in range(w_ref.shape[1] // 512):
                r = _dot(h, w_ref[:, j * 512:(j + 1) * 512])
                voq_ref[rows, base:base + 512] = (r if act is None else act(r)).astype(BF16)
                base += 512
        _store_time_major(kt_ref, rows, (_dot(h, wk_ref[...]) * k_scale).T.astype(BF16))
        _store_time_major(grow_ref, rows, (_dot(h, wg_ref[...]) + bg_ref[...]).T[0:grow_ref.shape[1], :])


def _lane_tile(a, n):
    return jnp.concatenate([a] * n, axis=1)


def _ml_core_kernel(voq_ref, kt_ref, grow_ref, nw_ref, triu_ref, x_ref, vec_ref, wout_ref,
                    out_ref, c_ref, m_ref, hs_ref, *, heads, dk, dv):
    L = CHUNK
    assert L == LANES

    @pl.when(pl.program_id(0) == 0)
    def _():
        c_ref[...] = jnp.zeros_like(c_ref)
        m_ref[...] = jnp.zeros_like(m_ref)

    row = lax.broadcasted_iota(jnp.int32, (L, L), 0)
    col = lax.broadcasted_iota(jnp.int32, (L, L), 1)
    causal = col <= row

    ones_blk = jnp.ones((L, LANES), BF16)
    sub = lax.broadcasted_iota(jnp.int32, (grow_ref.shape[0], L), 0)
    is_fg = (sub >= heads) & (sub < 2 * heads)

    dvt = heads * dv
    for c in range(voq_ref.shape[0] // L):
        t = slice(c * L, (c + 1) * L)
        gr = grow_ref[:, t]
        lf_all = jnp.where(is_fg, _log_sigmoid(gr), 0.0)
        b_all = _dot(jnp.concatenate(_split_bf16(lf_all, 3), axis=1), triu_ref[...])

        for h in range(heads):
            hv = slice(h * dv, (h + 1) * dv)
            q = voq_ref[t, 2 * dvt + h * dk:2 * dvt + (h + 1) * dk]
            kt = kt_ref[h * dk:(h + 1) * dk, t]
            v_aug = jnp.concatenate([voq_ref[t, hv], ones_blk], axis=1)
            lf_r = lf_all[heads + h:heads + h + 1, :]
            r = gr[h:h + 1, :] - b_all[heads + h:heads + h + 1, :]
            m_prev = m_ref[h:h + 1, :]

            b_c = jnp.sum(jnp.where(causal, lf_r, 0.0), axis=1, keepdims=True)
            r_mask = jnp.where(causal, r, -jnp.inf)
            g = jnp.maximum(jnp.max(r_mask, axis=1, keepdims=True), m_prev)
            s = _dot(q, kt) * jnp.exp(r_mask - g)
            inter = jnp.exp(m_prev - g)
            c_old = c_ref[h]
            numden = (_dot(s.astype(BF16), v_aug)
                      + _lane_tile(inter, dv // LANES + 1) * _dot(q, c_old.astype(BF16)))
            den = jnp.maximum(jnp.abs(numden[:, dv:]), jnp.exp(-(b_c + g)))
            hh = numden[:, :dv] * _lane_tile(1.0 / den, dv // LANES)

            g_last = g[L - 1:L, :]
            w_r = jnp.exp(r - g_last)
            decay = jnp.exp(m_prev - g_last)
            ktw = (kt.astype(F32) * w_r).astype(BF16)
            c_ref[h] = _lane_tile(decay, dv // LANES + 1) * c_old + _dot(ktw, v_aug)
            m_ref[h:h + 1, :] = jnp.sum(lf_r, axis=1, keepdims=True) + g_last

            hn = hh * lax.rsqrt(jnp.mean(hh * hh, axis=-1, keepdims=True) + EPS)
            gate = voq_ref[t, dvt + h * dv:dvt + (h + 1) * dv].astype(F32)
            hs_ref[t, hv] = (hn * nw_ref[:, hv] * gate).astype(BF16)

    for rows in _halves(x_ref):
        y = _dot(hs_ref[rows, :], wout_ref[...])
        out_ref[rows, :] = _postnorm_residual(x_ref[rows, :], y, vec_ref[...], 1.0)


def _mlstm_sublayer(x, vec, w_in16, layer, b_gate, norm_w, w_out16):
    s, d = x.shape
    heads = ML_HEADS
    qk = (w_in16.shape[2] - 2 * heads) // 6
    dvt = 2 * qk
    dk, dv = qk // heads, dvt // heads
    tm = min(IN_ROW_TILE, s)

    wg = jnp.pad(w_in16[layer, :, 2 * qk + 2 * dvt:], ((0, 0), (0, LANES - 2 * heads)))
    bg = jnp.pad(b_gate, (0, LANES - 2 * heads)).reshape(1, LANES)
    n_main = qk + 2 * dvt
    L = CHUNK
    blk = min(ML_CHUNKS_PER_STEP * L, s)
    kt_spec, kt_shape = _time_major_out(qk, tm, blk, BF16, s)
    g_spec, g_shape = _time_major_out(16, tm, blk, F32, s)

    voq, kt, grow = pl.pallas_call(
        functools.partial(_ml_in_kernel, k_scale=dk ** -0.5),
        grid=(s // tm,),
        in_specs=[_row_spec(tm, d), _const_spec((8, d)),
                  _layer_spec(layer, d, qk, 0), _layer_spec(layer, d, qk, 1),
                  _layer_spec(layer, d, dvt, 1), _layer_spec(layer, d, dvt, 2),
                  _const_spec((d, LANES)), _const_spec((1, LANES))],
        out_specs=[_row_spec(tm, n_main), kt_spec, g_spec],
        out_shape=[jax.ShapeDtypeStruct((s, n_main), BF16), kt_shape, g_shape],
        compiler_params=_params(),
        name="mlstm_in",
    )(x, vec, w_in16, w_in16, w_in16, w_in16, wg, bg)

    tri = jnp.arange(L)[None, :] >= jnp.arange(L)[:, None]
    triu3 = jnp.tile(tri, (3, 1)).astype(BF16)
    return pl.pallas_call(
        functools.partial(_ml_core_kernel, heads=heads, dk=dk, dv=dv),
        grid=(s // blk,),
        in_specs=[_row_spec(blk, n_main), _time_major_in(qk, blk),
                  _time_major_in(16, blk), _const_spec((1, dvt)), _const_spec((3 * L, L)),
                  _row_spec(blk, d), _const_spec((8, d)), _layer_spec(layer, dvt, d, 0)],
        out_specs=_row_spec(blk, d),
        out_shape=jax.ShapeDtypeStruct((s, d), F32),
        scratch_shapes=[pltpu.VMEM((heads, dk, dv + LANES), F32), pltpu.VMEM((8, LANES), F32),
                        pltpu.VMEM((blk, dvt), BF16)],
        compiler_params=_params(),
        name="mlstm_core",
    )(voq, kt, grow, norm_w.reshape(1, dvt), triu3, x, vec, w_out16)


def _mb_in_kernel(x_ref, vec_ref, wzx_ref, wd_ref, z_ref, xbc_ref, dtc_ref, dtr_ref):
    di = z_ref.shape[1]
    for rows in _halves(x_ref):
        h = _prenorm_mod(x_ref[rows, :], vec_ref[...]).astype(BF16)
        for j in range(di // 512):
            cols = slice(j * 512, (j + 1) * 512)
            z_ref[rows, cols] = _silu(_dot(h, wzx_ref[:, cols])).astype(BF16)
        for j in range(xbc_ref.shape[1] // 512):
            cols = slice(j * 512, (j + 1) * 512)
            xbc_ref[rows, cols] = _dot(h, wzx_ref[:, di + j * 512:di + (j + 1) * 512]).astype(BF16)
        dt = _dot(h, wd_ref[...])
        dtc_ref[rows, :] = dt
        _store_time_major(dtr_ref, rows, dt.T[0:dtr_ref.shape[1], :])


def _split_bf16(v, terms):
    parts = []
    for _ in range(terms):
        part = v.astype(BF16)
        parts.append(part)
        v = v - part.astype(F32)
    return parts


def _mb_core_kernel(z_ref, xbc_ref, dtc_ref, dtr_ref, cw_ref, cb_ref, dbc_ref, dbr_ref, ac_ref, ar_ref,
                    e_ref, dsk_ref, nw_ref, shift_ref, tril_ref, triu_ref, x_ref, vec_ref, wout_ref,
                    out_ref, xext_ref, xa_ref, st_ref, ys_ref, *, di, groups, nstate, hpg, p):
    Q = CHUNK

    @pl.when(pl.program_id(0) == 0)
    def _():
        xext_ref[Q:2 * Q, :] = jnp.zeros((Q, xext_ref.shape[1]), BF16)
        st_ref[...] = jnp.zeros_like(st_ref)

    for c in range(z_ref.shape[0] // Q):
        t = slice(c * Q, (c + 1) * Q)
        _mb_chunk(z_ref, xbc_ref, dtc_ref, dtr_ref, cw_ref, cb_ref, dbc_ref, dbr_ref, ac_ref, ar_ref,
                  e_ref, dsk_ref, nw_ref, shift_ref, tril_ref, triu_ref, xext_ref, xa_ref, st_ref, ys_ref,
                  t, di, groups, nstate, hpg, p)

    for rows in _halves(x_ref):
        y = _dot(ys_ref[rows, :], wout_ref[...])
        out_ref[rows, :] = _postnorm_residual(x_ref[rows, :], y, vec_ref[...], 1.0)


def _mb_chunk(z_ref, xbc_ref, dtc_ref, dtr_ref, cw_ref, cb_ref, dbc_ref, dbr_ref, ac_ref, ar_ref,
              e_ref, dsk_ref, nw_ref, shift_ref, tril_ref, triu_ref, xext_ref, xa_ref, st_ref, ys_ref,
              t, di, groups, nstate, hpg, p):
    Q = CHUNK
    gw = hpg * p
    conv_dim = di + 2 * groups * nstate

    xext_ref[0:Q, :] = xext_ref[Q:2 * Q, :]
    xext_ref[Q:2 * Q, :] = xbc_ref[t, :]

    def conv_silu(first_col, width):
        cols = slice(first_col, first_col + width)
        shifted = _dot(shift_ref[...], xext_ref[:, cols])
        acc = cb_ref[:, cols] + cw_ref[MB_CONV - 1:MB_CONV, cols] * xbc_ref[t, cols].astype(F32)
        for k in range(MB_CONV - 1):
            acc = acc + cw_ref[k:k + 1, cols] * shifted[k * Q:(k + 1) * Q]
        xa_ref[:, cols] = _silu(acc)

    for first_col in range(di, conv_dim, gw):
        conv_silu(first_col, gw)
    conv_silu(0, gw)

    row = lax.broadcasted_iota(jnp.int32, (Q, Q), 0)
    col = lax.broadcasted_iota(jnp.int32, (Q, Q), 1)
    causal = col <= row

    dt_c = _softplus(dtc_ref[t, :] + dbc_ref[...])
    dt_r = _softplus(dtr_ref[:, t] + dbr_ref[...])
    da_c = jnp.concatenate(_split_bf16(dt_c * -jnp.exp(ac_ref[...]), 3), axis=0)
    da_r = jnp.concatenate(_split_bf16(dt_r * -jnp.exp(ar_ref[...]), 3), axis=1)
    a_c = _dot(tril_ref[...], da_c)
    a_r = _dot(da_r, triu_ref[...])
    a_last = a_c[Q - 1:Q, :]
    per_head = jnp.concatenate([dt_c, jnp.exp(a_c), jnp.exp(a_last - a_c),
                                jnp.broadcast_to(jnp.exp(a_last), (16, LANES))], axis=0)
    per_head = jnp.concatenate(_split_bf16(per_head, 2), axis=1)
    lane = lax.broadcasted_iota(jnp.int32, (Q, LANES), 1)
    a2_c, a2_r = a_c * LOG2_E, a_r * LOG2_E

    for g in range(groups):
        gcols = slice(g * gw, (g + 1) * gw)
        if g + 1 < groups:
            conv_silu((g + 1) * gw, gw)
        per_chan = _dot(per_head, e_ref[:, gcols])
        dt_x, ea_x, dec_x = per_chan[0:Q], per_chan[Q:2 * Q], per_chan[2 * Q:3 * Q]
        eal = per_chan[3 * Q:3 * Q + 1]
        xs = xa_ref[:, gcols]
        bm = xa_ref[:, di + g * nstate:di + (g + 1) * nstate]
        cm = xa_ref[:, di + groups * nstate + g * nstate:di + groups * nstate + (g + 1) * nstate]
        cm16 = cm.astype(BF16)
        xdt = xs * dt_x
        xdt16 = xdt.astype(BF16)
        cb = jnp.where(causal, _dot_nt(cm16, bm.astype(BF16)), 0.0)

        pairs = []
        for hp in range(hpg // 2):
            x_pair = xdt16[:, hp * LANES:(hp + 1) * LANES]
            lmats = []
            for e in range(2):
                hd = g * hpg + hp * 2 + e
                seg = a2_c[:, hd:hd + 1] - a2_r[hd:hd + 1, :]
                lmats.append((cb * jnp.exp2(jnp.minimum(seg, 0.0))).astype(BF16))
            ys = _dot(jnp.concatenate(lmats, axis=0), x_pair)
            pairs.append(jnp.where(lane < p, ys[0:Q], ys[Q:2 * Q]))
        y = jnp.concatenate(pairs, axis=1)

        st = st_ref[g]
        y = y + _dot(cm16, st.astype(BF16)) * ea_x
        y = y + dsk_ref[:, gcols] * xs

        xdec = (xdt * dec_x).astype(BF16)
        st_ref[g] = eal * st + _dot(bm.T.astype(BF16), xdec)

        y = y * z_ref[t, gcols].astype(F32)
        y = y * lax.rsqrt(jnp.mean(y * y, axis=-1, keepdims=True) + EPS)
        ys_ref[t, gcols] = (y * nw_ref[:, gcols]).astype(BF16)


def _mamba_sublayer(x, vec, w_in16, layer, conv_w, conv_b, dt_bias, a_log, d_skip, norm_w, w_out16):
    s, d = x.shape
    heads = dt_bias.shape[0]
    p, groups, nstate = MB_HEADDIM, MB_GROUPS, MB_STATE
    hpg = heads // groups
    di = heads * p
    conv_dim = di + 2 * groups * nstate
    tm = min(IN_ROW_TILE, s)

    wd = jnp.pad(w_in16[layer, :, di + conv_dim:], ((0, 0), (0, LANES - heads)))
    Q = CHUNK
    blk = min(MB_CHUNKS_PER_STEP * Q, s)
    dtr_spec, dtr_shape = _time_major_out(heads, tm, blk, F32, s)

    z, xbc, dtc, dtr = pl.pallas_call(
        _mb_in_kernel,
        grid=(s // tm,),
        in_specs=[_row_spec(tm, d), _const_spec((8, d)), _layer_spec(layer, d, di + conv_dim, 0),
                  _const_spec((d, LANES))],
        out_specs=[_row_spec(tm, di), _row_spec(tm, conv_dim), _row_spec(tm, LANES), dtr_spec],
        out_shape=[jax.ShapeDtypeStruct((s, di), BF16), jax.ShapeDtypeStruct((s, conv_dim), BF16),
                   jax.ShapeDtypeStruct((s, LANES), F32), dtr_shape],
        compiler_params=_params(),
        name="mamba_in",
    )(x, vec, w_in16, wd)

    pad = (0, LANES - heads)
    head_of_chan = jnp.arange(di)[None, :] // p
    expand = jnp.tile(jnp.arange(LANES)[:, None] == head_of_chan, (2, 1)).astype(BF16)
    t_out = jnp.arange((MB_CONV - 1) * Q)
    src = Q + t_out % Q - (MB_CONV - 1) + t_out // Q
    shift = (jnp.arange(2 * Q)[None, :] == src[:, None]).astype(BF16)
    tri = jnp.arange(Q)[None, :] <= jnp.arange(Q)[:, None]
    tril3 = jnp.tile(tri, (1, 3)).astype(BF16)
    triu3 = jnp.tile(tri.T, (3, 1)).astype(BF16)
    return pl.pallas_call(
        functools.partial(_mb_core_kernel, di=di, groups=groups, nstate=nstate, hpg=hpg, p=p),
        grid=(s // blk,),
        in_specs=[_row_spec(blk, di), _row_spec(blk, conv_dim), _row_spec(blk, LANES),
                  _time_major_in(heads, blk),
                  _const_spec((MB_CONV, conv_dim)), _const_spec((1, conv_dim)),
                  _const_spec((1, LANES)), _const_spec((heads, 1)),
                  _const_spec((1, LANES)), _const_spec((heads, 1)),
                  _const_spec((2 * LANES, di)), _const_spec((1, di)), _const_spec((1, di)),
                  _const_spec(((MB_CONV - 1) * Q, 2 * Q)), _const_spec((Q, 3 * Q)), _const_spec((3 * Q, Q)),
                  _row_spec(blk, d), _const_spec((8, d)), _layer_spec(layer, di, d, 0)],
        out_specs=_row_spec(blk, d),
        out_shape=jax.ShapeDtypeStruct((s, d), F32),
        scratch_shapes=[pltpu.VMEM((2 * Q, conv_dim), BF16), pltpu.VMEM((Q, conv_dim), F32),
                        pltpu.VMEM((groups, nstate, hpg * p), F32), pltpu.VMEM((blk, di), BF16)],
        compiler_params=_params(),
        name="mamba_core",
    )(z, xbc, dtc, dtr, conv_w, conv_b.reshape(1, conv_dim),
      jnp.pad(dt_bias, pad).reshape(1, LANES), dt_bias.reshape(heads, 1),
      jnp.pad(a_log, pad).reshape(1, LANES), a_log.reshape(heads, 1),
      expand, jnp.repeat(d_skip, p).reshape(1, di), norm_w.reshape(1, di), shift, tril3, triu3,
      x, vec, w_out16)


def kernel(x, c, ada_w, ada_b, norm_pre, norm_post, ffn_w1, ffn_w3, ffn_w2, ml_w_in, ml_b_gate, ml_norm_w,
           ml_w_out, mb_w_in, mb_conv_w, mb_conv_b, mb_dt_bias, mb_A_log, mb_D, mb_norm_w, mb_w_out):
    batch, seq, d = x.shape
    assert batch == 1, "the kernels assume one sequence"
    depth = ada_w.shape[0]
    mod = _ada_mod(c, ada_w, ada_b).reshape(depth, N_SUB, 3, d)
    pad_rows = jnp.zeros((3, d), F32)
    d_ff = ffn_w1.shape[-1]
    w1 = ffn_w1.reshape(depth * 2, d, d_ff)
    w3 = ffn_w3.reshape(depth * 2, d, d_ff)
    w2 = ffn_w2.astype(BF16).reshape(depth * 2, d_ff, d)
    ml_in16, ml_out16 = ml_w_in.astype(BF16), ml_w_out.astype(BF16)
    mb_in16, mb_out16 = mb_w_in.astype(BF16), mb_w_out.astype(BF16)
    h = x.reshape(seq, d)
    for i in range(depth):
        j = i // 2
        for sub in range(N_SUB):
            vec = jnp.concatenate([norm_pre[i, sub][None], mod[i, sub, 0][None], mod[i, sub, 1][None],
                                   norm_post[i, sub][None], mod[i, sub, 2][None], pad_rows], axis=0)
            if sub != 1:
                h = _ffn_sublayer(h, vec, w1, w3, w2, 2 * i + sub // 2, FFN_HALF)
            elif i % 2 == 0:
                h = _mlstm_sublayer(h, vec, ml_in16, j, ml_b_gate[j], ml_norm_w[j], ml_out16)
            else:
                h = _mamba_sublayer(h, vec, mb_in16, j, mb_conv_w[j], mb_conv_b[j], mb_dt_bias[j],
                                    mb_A_log[j], mb_D[j], mb_norm_w[j], mb_out16)
    return h.reshape(batch, seq, d)
```
